```python
import jax, jax.numpy as jnp
from jax import lax
import numpy as np

D_MODEL = 1024
BATCH = 4
SEQ = 4096
DEPTH = 1

ATTN_HEADS = 16
ATTN_KV_HEADS = 4
ATTN_HEAD_DIM = 64
ATTN_GROUPS = ATTN_HEADS // ATTN_KV_HEADS
ATTN_WIDTH = ATTN_HEADS * ATTN_HEAD_DIM
ATTN_KV_WIDTH = ATTN_KV_HEADS * ATTN_HEAD_DIM
WINDOW = 128
ATTN_BLOCK = 128

MLSTM_HEADS = 8
MLSTM_QK_DIM = 64
MLSTM_V_DIM = 128
MLSTM_QK_WIDTH = MLSTM_HEADS * MLSTM_QK_DIM
MLSTM_V_WIDTH = MLSTM_HEADS * MLSTM_V_DIM
MLSTM_CHUNK = 64
CONV_WIDTH = 4
FORGET_BIAS_LO = 3.0
FORGET_BIAS_HI = 6.0
M_INIT = -1e30

RMS_EPS = 1e-6

SPLIT_SIZES = (ATTN_WIDTH, ATTN_KV_WIDTH, ATTN_KV_WIDTH, ATTN_WIDTH,
               MLSTM_QK_WIDTH, MLSTM_QK_WIDTH, MLSTM_V_WIDTH, MLSTM_HEADS, MLSTM_HEADS,
               MLSTM_V_WIDTH, MLSTM_V_WIDTH, D_MODEL, D_MODEL)
SPLIT_IDX = tuple(sum(SPLIT_SIZES[:i + 1]) for i in range(len(SPLIT_SIZES) - 1))
PROJ_WIDTH = sum(SPLIT_SIZES)
F_GATE_OFFSET = sum(SPLIT_SIZES[:8])

kernel_name = 'hybrid_swa_sink_mlstm_gated_merge'


def rms_norm(x, w):
    xf = x.astype(jnp.float32)
    y = xf * lax.rsqrt(jnp.mean(xf * xf, axis=-1, keepdims=True) + RMS_EPS)
    return (y * w.astype(jnp.float32)).astype(x.dtype)


def head_rms_norm(h, w):
    y = h * lax.rsqrt(jnp.mean(h * h, axis=-1, keepdims=True) + RMS_EPS)
    return y * w.astype(jnp.float32).reshape(MLSTM_HEADS, MLSTM_V_DIM)


def sliding_window_sink_attention(q, k, v, sinks):
    B, T = q.shape[0], q.shape[1]
    nb = T // ATTN_BLOCK
    qb = q.reshape(B, nb, ATTN_BLOCK, ATTN_KV_HEADS, ATTN_GROUPS, ATTN_HEAD_DIM)
    kb = k.reshape(B, nb, ATTN_BLOCK, ATTN_KV_HEADS, ATTN_HEAD_DIM)
    vb = v.reshape(B, nb, ATTN_BLOCK, ATTN_KV_HEADS, ATTN_HEAD_DIM)

    def with_prev(t):
        prev = jnp.concatenate([jnp.zeros_like(t[:, :1]), t[:, :-1]], axis=1)
        return jnp.concatenate([prev, t], axis=2)

    kw, vw = with_prev(kb), with_prev(vb)
    scores = jnp.einsum('bnqkgd,bnskd->bnkgqs', qb, kw).astype(jnp.float32) * (ATTN_HEAD_DIM ** -0.5)
    blk = jnp.arange(nb)[:, None] * ATTN_BLOCK
    q_pos = blk + jnp.arange(ATTN_BLOCK)[None, :]
    k_pos = blk - ATTN_BLOCK + jnp.arange(2 * ATTN_BLOCK)[None, :]
    rel = q_pos[:, :, None] - k_pos[:, None, :]
    allowed = (rel >= 0) & (rel < WINDOW) & (k_pos[:, None, :] >= 0)
    scores = jnp.where(allowed[None, :, None, None], scores, -jnp.inf)
    sink = sinks.astype(jnp.float32).reshape(ATTN_KV_HEADS, ATTN_GROUPS)[None, None, :, :, None, None]
    sink = jnp.broadcast_to(sink, scores.shape[:-1] + (1,))
    probs = jax.nn.softmax(jnp.concatenate([scores, sink], axis=-1), axis=-1)[..., :-1]
    out = jnp.einsum('bnkgqs,bnskd->bnqkgd', probs.astype(v.dtype), vw)
    return out.reshape(B, T, ATTN_WIDTH)


def causal_depthwise_conv(u, w, b):
    out = lax.conv_general_dilated(u, w[:, None, :], window_strides=(1,),
                                   padding=[(CONV_WIDTH - 1, 0)],
                                   dimension_numbers=('NWC', 'WIO', 'NWC'),
                                   feature_group_count=u.shape[-1])
    return out + b


def mlstm_chunkwise(q, k, v, i_pre, f_pre):
    B, T, H, dk = q.shape
    dv = v.shape[-1]
    nc = T // MLSTM_CHUNK

    def chunked(t):
        return jnp.moveaxis(t.reshape((B, nc, MLSTM_CHUNK) + t.shape[2:]), 3, 1)

    q = chunked(q) * (dk ** -0.5)
    k = chunked(k)
    v = chunked(v)
    ig = chunked(i_pre)
    logf = jax.nn.log_sigmoid(chunked(f_pre))
    b = jnp.cumsum(logf, axis=-1)
    causal = jnp.tril(jnp.ones((MLSTM_CHUNK, MLSTM_CHUNK), dtype=bool))
    log_d = jnp.where(causal, b[..., :, None] - b[..., None, :] + ig[..., None, :], -jnp.inf)

    g = b[..., -1]
    a = g[..., None] - b + ig
    m_loc = jnp.max(a, axis=-1)
    wts = jnp.exp(a - m_loc[..., None])
    c_loc = jnp.einsum('bhcl,bhclv,bhclk->bhcvk', wts, v, k)
    n_loc = jnp.einsum('bhcl,bhclk->bhck', wts, k)

    def step(carry, inp):
        c, n, m = carry
        g_c, m_l, c_l, n_l = inp
        m_new = jnp.maximum(g_c + m, m_l)
        s_prev = jnp.exp(g_c + m - m_new)
        s_loc = jnp.exp(m_l - m_new)
        c_new = s_prev[..., None, None] * c + s_loc[..., None, None] * c_l
        n_new = s_prev[..., None] * n + s_loc[..., None] * n_l
        return (c_new, n_new, m_new), (c, n, m)

    init = (jnp.zeros((B, H, dv, dk), q.dtype), jnp.zeros((B, H, dk), q.dtype),
            jnp.full((B, H), M_INIT, q.dtype))
    xs = (jnp.moveaxis(g, 2, 0), jnp.moveaxis(m_loc, 2, 0),
          jnp.moveaxis(c_loc, 2, 0), jnp.moveaxis(n_loc, 2, 0))
    _, (c_prev, n_prev, m_prev) = lax.scan(step, init, xs)
    c_prev = jnp.moveaxis(c_prev, 0, 2)
    n_prev = jnp.moveaxis(n_prev, 0, 2)
    m_prev = jnp.moveaxis(m_prev, 0, 2)

    m_inter = b + m_prev[..., None]
    m_row = jnp.maximum(m_inter, jnp.max(log_d, axis=-1))
    s = jnp.einsum('bhclk,bhcsk->bhcls', q, k) * jnp.exp(log_d - m_row[..., None])
    inter = jnp.exp(m_inter - m_row)
    num = jnp.einsum('bhcls,bhcsv->bhclv', s, v) + inter[..., None] * jnp.einsum('bhcvk,bhclk->bhclv', c_prev, q)
    den = jnp.sum(s, axis=-1) + inter * jnp.einsum('bhck,bhclk->bhcl', n_prev, q)
    h = num / jnp.maximum(jnp.abs(den), jnp.exp(-m_row))[..., None]
    return jnp.moveaxis(h, 1, 3).reshape(B, T, H, dv)


def hybrid_layer(x, norm_w, w_in, b_in, attn_sinks, conv_w, conv_b, mlstm_norm_w,
                 w_branch_attn, w_branch_mlstm, w_out):
    B, T, _ = x.shape
    f32 = jnp.float32
    h = rms_norm(x, norm_w)
    proj = jnp.einsum('btd,de->bte', h, w_in) + b_in
    (a_q, a_k, a_v, a_z, m_q, m_k, m_v, m_i, m_f, m_o, m_z,
     g_attn, g_mlstm) = jnp.split(proj, SPLIT_IDX, axis=-1)

    attn = sliding_window_sink_attention(a_q.reshape(B, T, ATTN_HEADS, ATTN_HEAD_DIM),
                                         a_k.reshape(B, T, ATTN_KV_HEADS, ATTN_HEAD_DIM),
                                         a_v.reshape(B, T, ATTN_KV_HEADS, ATTN_HEAD_DIM),
                                         attn_sinks)
    branch_a = jnp.einsum('bte,ed->btd', attn * jax.nn.silu(a_z), w_branch_attn)

    qk = jax.nn.silu(causal_depthwise_conv(jnp.concatenate([m_q, m_k], axis=-1), conv_w, conv_b))
    mq, mk = jnp.split(qk, 2, axis=-1)
    h_tilde = mlstm_chunkwise(mq.reshape(B, T, MLSTM_HEADS, MLSTM_QK_DIM).astype(f32),
                              mk.reshape(B, T, MLSTM_HEADS, MLSTM_QK_DIM).astype(f32),
                              m_v.reshape(B, T, MLSTM_HEADS, MLSTM_V_DIM).astype(f32),
                              m_i.astype(f32), m_f.astype(f32))
    h_m = head_rms_norm(h_tilde, mlstm_norm_w).reshape(B, T, MLSTM_V_WIDTH).astype(x.dtype)
    h_m = jax.nn.sigmoid(m_o) * h_m * jax.nn.silu(m_z)
    branch_m = jnp.einsum('bte,ed->btd', h_m, w_branch_mlstm)

    merged = jax.nn.sigmoid(g_attn) * branch_a + jax.nn.sigmoid(g_mlstm) * branch_m
    return x + jnp.einsum('btd,de->bte', merged, w_out)


def setup_inputs(seed: int = 0) -> dict:
    key = jax.random.key(seed)
    ks = jax.random.split(key, 12)
    nrm = jax.random.normal
    x = nrm(ks[0], (BATCH, SEQ, D_MODEL), jnp.float32)
    norm_w = 1.0 + 0.02 * nrm(ks[1], (DEPTH, D_MODEL), jnp.float32)
    w_in = nrm(ks[2], (DEPTH, D_MODEL, PROJ_WIDTH), jnp.float32) * (D_MODEL ** -0.5)
    b_in = 0.01 * nrm(ks[3], (DEPTH, PROJ_WIDTH), jnp.float32)
    b_in = b_in.at[:, F_GATE_OFFSET:F_GATE_OFFSET + MLSTM_HEADS].add(
        jnp.linspace(FORGET_BIAS_LO, FORGET_BIAS_HI, MLSTM_HEADS, dtype=jnp.float32))
    attn_sinks = 0.5 * nrm(ks[4], (DEPTH, ATTN_HEADS), jnp.float32)
    conv_w = nrm(ks[5], (DEPTH, CONV_WIDTH, 2 * MLSTM_QK_WIDTH), jnp.float32) * (CONV_WIDTH ** -0.5)
    conv_b = 0.01 * nrm(ks[6], (DEPTH, 2 * MLSTM_QK_WIDTH), jnp.float32)
    mlstm_norm_w = 1.0 + 0.02 * nrm(ks[7], (DEPTH, MLSTM_V_WIDTH), jnp.float32)
    w_branch_attn = nrm(ks[8], (DEPTH, ATTN_WIDTH, D_MODEL), jnp.float32) * (ATTN_WIDTH ** -0.5)
    w_branch_mlstm = nrm(ks[9], (DEPTH, MLSTM_V_WIDTH, D_MODEL), jnp.float32) * (MLSTM_V_WIDTH ** -0.5)
    w_out = nrm(ks[10], (DEPTH, D_MODEL, D_MODEL), jnp.float32) * (D_MODEL ** -0.5)
    final_norm_w = 1.0 + 0.02 * nrm(ks[11], (D_MODEL,), jnp.float32)
    return {'x': x, 'norm_w': norm_w, 'w_in': w_in, 'b_in': b_in, 'attn_sinks': attn_sinks,
            'conv_w': conv_w, 'conv_b': conv_b, 'mlstm_norm_w': mlstm_norm_w,
            'w_branch_attn': w_branch_attn, 'w_branch_mlstm': w_branch_mlstm,
            'w_out': w_out, 'final_norm_w': final_norm_w}


def reference(x, norm_w, w_in, b_in, attn_sinks, conv_w, conv_b, mlstm_norm_w,
              w_branch_attn, w_branch_mlstm, w_out, final_norm_w):
    for layer in range(DEPTH):
        x = hybrid_layer(x, norm_w[layer], w_in[layer], b_in[layer], attn_sinks[layer],
                         conv_w[layer], conv_b[layer], mlstm_norm_w[layer],
                         w_branch_attn[layer], w_branch_mlstm[layer], w_out[layer])
    return rms_norm(x, final_norm_w)
```

```python
import functools

import jax
import jax.numpy as jnp
from jax import lax
from jax.experimental import pallas as pl
from jax.experimental.pallas import tpu as pltpu

F32 = jnp.float32
BF16 = jnp.bfloat16

D_MODEL = 1024
ATTN_HEADS = 16
ATTN_KV_HEADS = 4
ATTN_GROUPS = ATTN_HEADS // ATTN_KV_HEADS
ATTN_HEAD_DIM = 64
ATTN_BLOCK = 128
MLSTM_HEADS = 8
MLSTM_QK_DIM = 64
MLSTM_V_DIM = 128
MLSTM_CHUNK = 128
CONV_WIDTH = 4
CONV_HALO = 8
M_INIT = -1e30
RMS_EPS = 1e-6

_SRC = dict(a_q=(0, 1024), a_k=(1024, 256), a_v=(1280, 256), a_z=(1536, 1024),
            m_q=(2560, 512), m_k=(3072, 512), m_v=(3584, 1024), gates=(4608, 16),
            m_o=(4624, 1024), m_z=(5648, 1024), g_a=(6672, 1024), g_m=(7696, 1024))
_ORDER = ("a_q", "a_z", "m_v", "m_o", "m_z", "g_a", "g_m", "m_q", "m_k", "a_k", "a_v")
_OFF = {}
_o = 0
for _name in _ORDER:
    _OFF[_name] = _o
    _o += _SRC[_name][1]
N_MAIN = _o
GATE_PAD = 128
N_CHUNK = 512

V7X_VMEM_BYTES = 64 * 1024 * 1024


def _blk(name):
    off, width = _OFF[name], _SRC[name][1]
    assert off % width == 0
    return off // width


def _vmem_limit(estimate_bytes):
    return int(min(V7X_VMEM_BYTES - (4 << 20), estimate_bytes + (8 << 20)))


def _sigmoid(x):
    return jax.nn.sigmoid(x)


def _silu(x):
    return x * jax.nn.sigmoid(x)


def _inproj_kernel(x_ref, nw_ref, w_ref, b_ref, wg_ref, bg_ref, p_ref, g_ref):
    x = x_ref[...]
    ms = jnp.mean(x * x, axis=-1, keepdims=True)
    h = (x * lax.rsqrt(ms + RMS_EPS) * nw_ref[...]).astype(BF16)
    for c in range(0, N_MAIN, N_CHUNK):
        acc = jnp.dot(h, w_ref[:, c:c + N_CHUNK], preferred_element_type=F32)
        p_ref[:, c:c + N_CHUNK] = (acc + b_ref[:, c:c + N_CHUNK]).astype(BF16)
    g_ref[...] = jnp.dot(h, wg_ref[...], preferred_element_type=F32) + bg_ref[...]


def _inproj(x2d, norm_w, w_main, b_main, w_gate, b_gate, *, tm):
    m = x2d.shape[0]
    est = (2 * tm * D_MODEL * 4 + D_MODEL * N_MAIN * 2 + 2 * tm * N_MAIN * 2
           + 2 * D_MODEL * GATE_PAD * 2 + 2 * tm * GATE_PAD * 4 + tm * N_CHUNK * 4 * 4)
    const = lambda i: (0, 0)
    return pl.pallas_call(
        _inproj_kernel,
        grid=(m // tm,),
        in_specs=[
            pl.BlockSpec((tm, D_MODEL), lambda i: (i, 0)),
            pl.BlockSpec((1, D_MODEL), const),
            pl.BlockSpec((D_MODEL, N_MAIN), const, pipeline_mode=pl.Buffered(1)),
            pl.BlockSpec((1, N_MAIN), const),
            pl.BlockSpec((D_MODEL, GATE_PAD), const),
            pl.BlockSpec((1, GATE_PAD), const),
        ],
        out_specs=[
            pl.BlockSpec((tm, N_MAIN), lambda i: (i, 0)),
            pl.BlockSpec((tm, GATE_PAD), lambda i: (i, 0)),
        ],
        out_shape=[
            jax.ShapeDtypeStruct((m, N_MAIN), BF16),
            jax.ShapeDtypeStruct((m, GATE_PAD), F32),
        ],
        compiler_params=pltpu.CompilerParams(
            dimension_semantics=("parallel",), vmem_limit_bytes=_vmem_limit(est)),
        name="inproj",
    )(x2d, norm_w, w_main, b_main, w_gate, b_gate)


def _attn_kernel(sink_ref, q_ref, kc_ref, kp_ref, vc_ref, vp_ref, z_ref, o_ref, *, tq):
    n = pl.program_id(1)
    blk = ATTN_BLOCK
    hd = ATTN_HEAD_DIM
    row = lax.broadcasted_iota(jnp.int32, (ATTN_GROUPS * blk, 2 * blk), 0) & (blk - 1)
    col = lax.broadcasted_iota(jnp.int32, (ATTN_GROUPS * blk, 2 * blk), 1)
    in_window = ((col < blk) & (col > row)) | ((col >= blk) & ((col - blk) <= row))
    first_key = jnp.where(n > 0, 0, blk)
    for s in range(tq // blk):
        rows = slice(s * blk, (s + 1) * blk)
        if s == 0:
            k_prev, v_prev = kp_ref[...], vp_ref[...]
            allowed_g = in_window & (col >= first_key)
        else:
            prev_rows = slice((s - 1) * blk, s * blk)
            k_prev, v_prev = kc_ref[prev_rows, :], vc_ref[prev_rows, :]
            allowed_g = in_window
        keys = jnp.concatenate([k_prev, kc_ref[rows, :]], axis=0)
        vals = jnp.concatenate([v_prev, vc_ref[rows, :]], axis=0)
        for kv in range(ATTN_KV_HEADS):
            heads = [kv * ATTN_GROUPS + j for j in range(ATTN_GROUPS)]
            q_g = jnp.concatenate(
                [q_ref[rows, h * hd:(h + 1) * hd] for h in heads], axis=0)
            sink = jnp.concatenate(
                [jnp.full((blk, 1), sink_ref[h], F32) for h in heads], axis=0)
            sc = lax.dot_general(q_g, keys[:, kv * hd:(kv + 1) * hd],
                                 (((1,), (1,)), ((), ())),
                                 preferred_element_type=F32) * (hd ** -0.5)
            sc = jnp.where(allowed_g, sc, -jnp.inf)
            m = jnp.maximum(jnp.max(sc, axis=-1, keepdims=True), sink)
            p = jnp.exp(sc - m)
            den = jnp.sum(p, axis=-1, keepdims=True) + jnp.exp(sink - m)
            o_g = jnp.dot(p.astype(BF16), vals[:, kv * hd:(kv + 1) * hd],
                          preferred_element_type=F32) / den
            o_cat = jnp.concatenate(
                [o_g[j * blk:(j + 1) * blk, :] for j in range(ATTN_GROUPS)], axis=1)
            cols = slice(kv * ATTN_GROUPS * hd, (kv + 1) * ATTN_GROUPS * hd)
            z = z_ref[rows, cols].astype(F32)
            o_ref[rows, cols] = (o_cat * _silu(z)).astype(BF16)


def _attn(p, sinks, *, batch, seq, tq):
    m = batch * seq
    nt = seq // tq
    sub = tq // ATTN_BLOCK
    kvw = ATTN_KV_HEADS * ATTN_HEAD_DIM
    cur = lambda b, n: b * nt + n
    prev = lambda b, n: jnp.maximum((b * nt + n) * sub - 1, 0)
    est = 2 * (2 * tq * D_MODEL * 2 + 2 * tq * kvw * 2 + 2 * ATTN_BLOCK * kvw * 2
               + tq * D_MODEL * 2) + (8 << 20)
    return pl.pallas_call(
        functools.partial(_attn_kernel, tq=tq),
        grid=(batch, nt),
        in_specs=[
            pl.BlockSpec(memory_space=pltpu.SMEM),
            pl.BlockSpec((tq, D_MODEL), lambda b, n: (cur(b, n), _blk("a_q"))),
            pl.BlockSpec((tq, kvw), lambda b, n: (cur(b, n), _blk("a_k"))),
            pl.BlockSpec((ATTN_BLOCK, kvw), lambda b, n: (prev(b, n), _blk("a_k"))),
            pl.BlockSpec((tq, kvw), lambda b, n: (cur(b, n), _blk("a_v"))),
            pl.BlockSpec((ATTN_BLOCK, kvw), lambda b, n: (prev(b, n), _blk("a_v"))),
            pl.BlockSpec((tq, D_MODEL), lambda b, n: (cur(b, n), _blk("a_z"))),
        ],
        out_specs=pl.BlockSpec((tq, D_MODEL), lambda b, n: (cur(b, n), 0)),
        out_shape=jax.ShapeDtypeStruct((m, D_MODEL), BF16),
        compiler_params=pltpu.CompilerParams(
            dimension_semantics=("parallel", "arbitrary"), vmem_limit_bytes=_vmem_limit(est)),
        name="swa_attn",
    )(sinks, p, p, p, p, p, p)


def _log_sigmoid(x):
    return jnp.minimum(x, 0.0) - jnp.log1p(jnp.exp(-jnp.abs(x)))


def _mlstm_kernel(q_ref, k_ref, v_ref, g_ref, o_ref, z_ref, cw_ref, cb_ref, nw_ref, y_ref,
                  ubuf, c_st, n_st, m_st):
    L = MLSTM_CHUNK
    H = MLSTM_HEADS
    dk = MLSTM_QK_DIM
    dv = MLSTM_V_DIM
    qkw = H * dk

    @pl.when(pl.program_id(1) == 0)
    def _():
        ubuf[0:CONV_HALO, :] = jnp.zeros((CONV_HALO, 2 * qkw), F32)
        c_st[...] = jnp.zeros(c_st.shape, F32)
        n_st[...] = jnp.zeros(n_st.shape, F32)
        m_st[...] = jnp.full(m_st.shape, M_INIT, F32)

    ubuf[CONV_HALO:CONV_HALO + L, 0:qkw] = q_ref[...].astype(F32)
    ubuf[CONV_HALO:CONV_HALO + L, qkw:2 * qkw] = k_ref[...].astype(F32)
    first = CONV_HALO - (CONV_WIDTH - 1)
    conv = cw_ref[0:1, :] * ubuf[first:first + L, :]
    for j in range(1, CONV_WIDTH):
        conv = conv + cw_ref[j:j + 1, :] * ubuf[first + j:first + j + L, :]
    qk = _silu(conv + cb_ref[...])
    ubuf[0:CONV_HALO, :] = ubuf[L:L + CONV_HALO, :]

    g_t = g_ref[...].T
    i_row = g_t[0:H, :]
    logf = _log_sigmoid(g_t[H:2 * H, :])
    lane = lax.broadcasted_iota(jnp.int32, (H, L), 1)
    b_row = logf
    sh = 1
    while sh < L:
        b_row = b_row + jnp.where(lane >= sh, pltpu.roll(b_row, sh, 1), 0.0)
        sh *= 2
    g_tot = b_row[:, L - 1:L]
    m_prev = m_st[:, 0:1]
    a_row = g_tot - b_row + i_row
    m_new = jnp.maximum(g_tot + m_prev, jnp.max(a_row, axis=-1, keepdims=True))
    s_prev = jnp.exp(g_tot + m_prev - m_new)
    w_row = jnp.exp(a_row - m_new)
    m_st[...] = jnp.broadcast_to(m_new, m_st.shape)
    stacked = jnp.concatenate(
        [b_row, w_row, jnp.zeros((GATE_PAD - 2 * H, L), F32)], axis=0)
    cols = stacked.T

    t_idx = lax.broadcasted_iota(jnp.int32, (L, L), 0)
    s_idx = lax.broadcasted_iota(jnp.int32, (L, L), 1)
    causal = s_idx <= t_idx

    for h in range(H):
        q_h = qk[:, h * dk:(h + 1) * dk] * (dk ** -0.5)
        k_h = qk[:, qkw + h * dk:qkw + (h + 1) * dk]
        v_h = v_ref[:, h * dv:(h + 1) * dv]
        q_b = q_h.astype(BF16)
        b_col = cols[:, h:h + 1]
        w_col = cols[:, H + h:H + h + 1]
        log_d = jnp.where(causal, b_col - b_row[h:h + 1, :] + i_row[h:h + 1, :], -jnp.inf)
        m_inter = b_col + m_prev[h:h + 1, :]
        m_row = jnp.maximum(m_inter, jnp.max(log_d, axis=-1, keepdims=True))
        qk_t = lax.dot_general(q_b, k_h.astype(BF16), (((1,), (1,)), ((), ())),
                               preferred_element_type=F32)
        s_mat = qk_t * jnp.exp(log_d - m_row)
        inter = jnp.exp(m_inter - m_row)
        c_h = c_st[h]
        n_h = n_st[h:h + 1, :]
        num = (jnp.dot(s_mat.astype(BF16), v_h, preferred_element_type=F32)
               + inter * jnp.dot(q_b, c_h.astype(BF16), preferred_element_type=F32))
        den = (jnp.sum(s_mat, axis=-1, keepdims=True)
               + inter * jnp.sum(q_h * n_h, axis=-1, keepdims=True))
        h_t = num / jnp.maximum(jnp.abs(den), jnp.exp(-m_row))

        kw = k_h * w_col
        c_st[h] = s_prev[h:h + 1, :] * c_h + lax.dot_general(
            kw.astype(BF16), v_h, (((0,), (0,)), ((), ())), preferred_element_type=F32)
        n_st[h:h + 1, :] = s_prev[h:h + 1, :] * n_h + jnp.sum(kw, axis=0, keepdims=True)

        cols_h = slice(h * dv, (h + 1) * dv)
        ms = jnp.mean(h_t * h_t, axis=-1, keepdims=True)
        y = h_t * lax.rsqrt(ms + RMS_EPS) * nw_ref[:, cols_h]
        og = o_ref[:, cols_h].astype(F32)
        zg = z_ref[:, cols_h].astype(F32)
        y_ref[:, cols_h] = (_sigmoid(og) * y * _silu(zg)).astype(BF16)


def _mlstm(p, gates, conv_w, conv_b, norm_w, *, batch, seq):
    m = batch * seq
    L = MLSTM_CHUNK
    nt = seq // L
    qkw = MLSTM_HEADS * MLSTM_QK_DIM
    vw = MLSTM_HEADS * MLSTM_V_DIM
    cur = lambda b, n: b * nt + n
    const = lambda b, n: (0, 0)
    est = 2 * (2 * L * qkw * 2 + 3 * L * vw * 2 + L * GATE_PAD * 4 + L * vw * 2) + (16 << 20)
    return pl.pallas_call(
        _mlstm_kernel,
        grid=(batch, nt),
        in_specs=[
            pl.BlockSpec((L, qkw), lambda b, n: (cur(b, n), _blk("m_q"))),
            pl.BlockSpec((L, qkw), lambda b, n: (cur(b, n), _blk("m_k"))),
            pl.BlockSpec((L, vw), lambda b, n: (cur(b, n), _blk("m_v"))),
            pl.BlockSpec((L, GATE_PAD), lambda b, n: (cur(b, n), 0)),
            pl.BlockSpec((L, vw), lambda b, n: (cur(b, n), _blk("m_o"))),
            pl.BlockSpec((L, vw), lambda b, n: (cur(b, n), _blk("m_z"))),
            pl.BlockSpec((CONV_WIDTH, 2 * qkw), const),
            pl.BlockSpec((1, 2 * qkw), const),
            pl.BlockSpec((1, vw), const),
        ],
        out_specs=pl.BlockSpec((L, vw), lambda b, n: (cur(b, n), 0)),
        out_shape=jax.ShapeDtypeStruct((m, vw), BF16),
        scratch_shapes=[
            pltpu.VMEM((CONV_HALO + L, 2 * qkw), F32),
            pltpu.VMEM((MLSTM_HEADS, MLSTM_QK_DIM, MLSTM_V_DIM), F32),
            pltpu.VMEM((MLSTM_HEADS, MLSTM_QK_DIM), F32),
            pltpu.VMEM((MLSTM_HEADS, 128), F32),
        ],
        compiler_params=pltpu.CompilerParams(
            dimension_semantics=("parallel", "arbitrary"), vmem_limit_bytes=_vmem_limit(est)),
        name="mlstm",
    )(p, p, p, gates, p, p, conv_w, conv_b, norm_w)


def _outproj_kernel(ya_ref, ym_ref, ga_ref, gm_ref, x_ref, wa_ref, wm_ref, wo_ref, fw_ref,
                    out_ref):
    br_a = jnp.dot(ya_ref[...], wa_ref[...], preferred_element_type=F32)
    br_m = jnp.dot(ym_ref[...], wm_ref[...], preferred_element_type=F32)
    merged = (_sigmoid(ga_ref[...].astype(F32)) * br_a
              + _sigmoid(gm_ref[...].astype(F32)) * br_m)
    r = x_ref[...] + jnp.dot(merged.astype(BF16), wo_ref[...], preferred_element_type=F32)
    ms = jnp.mean(r * r, axis=-1, keepdims=True)
    out_ref[...] = r * lax.rsqrt(ms + RMS_EPS) * fw_ref[...]


def _outproj(ya, ym, p, x2d, wa, wm, wo, fw, *, tm):
    m = x2d.shape[0]
    row = lambda i: (i, 0)
    const = lambda i: (0, 0)
    est = (2 * (4 * tm * D_MODEL * 2 + 2 * tm * D_MODEL * 4 + 3 * D_MODEL * D_MODEL * 2)
           + 4 * tm * D_MODEL * 4)
    return pl.pallas_call(
        _outproj_kernel,
        grid=(m // tm,),
        in_specs=[
            pl.BlockSpec((tm, D_MODEL), row),
            pl.BlockSpec((tm, D_MODEL), row),
            pl.BlockSpec((tm, D_MODEL), lambda i: (i, _blk("g_a"))),
            pl.BlockSpec((tm, D_MODEL), lambda i: (i, _blk("g_m"))),
            pl.BlockSpec((tm, D_MODEL), row),
            pl.BlockSpec((D_MODEL, D_MODEL), const),
            pl.BlockSpec((D_MODEL, D_MODEL), const),
            pl.BlockSpec((D_MODEL, D_MODEL), const),
            pl.BlockSpec((1, D_MODEL), const),
        ],
        out_specs=pl.BlockSpec((tm, D_MODEL), row),
        out_shape=jax.ShapeDtypeStruct((m, D_MODEL), F32),
        compiler_params=pltpu.CompilerParams(
            dimension_semantics=("parallel",), vmem_limit_bytes=_vmem_limit(est)),
        name="outproj",
    )(ya, ym, p, p, x2d, wa, wm, wo, fw)


def _layer(x, norm_w, w_in, b_in, sinks, conv_w, conv_b, mnorm_w, w_ba, w_bm, w_out, final_w,
           *, final):
    batch, seq, d = x.shape
    assert d == D_MODEL and seq % 512 == 0
    assert w_in.shape == (D_MODEL, _SRC["g_m"][0] + _SRC["g_m"][1])
    m = batch * seq
    x2d = x.reshape(m, d)

    def seg(a, name):
        off, width = _SRC[name]
        return a[..., off:off + width]

    w_main = jnp.concatenate([seg(w_in, nm) for nm in _ORDER], axis=-1).astype(BF16)
    b_main = jnp.concatenate([seg(b_in, nm) for nm in _ORDER], axis=-1)[None, :]
    pad = GATE_PAD - _SRC["gates"][1]
    w_gate = jnp.pad(seg(w_in, "gates"), ((0, 0), (0, pad))).astype(BF16)
    b_gate = jnp.pad(seg(b_in, "gates"), ((0, pad),))[None, :]

    p, gates = _inproj(x2d, norm_w[None, :], w_main, b_main, w_gate, b_gate, tm=512)
    ya = _attn(p, sinks, batch=batch, seq=seq, tq=256)
    ym = _mlstm(p, gates, conv_w, conv_b[None, :], mnorm_w[None, :], batch=batch, seq=seq)
    fw = final_w[None, :] if final else None
    assert final, "single-layer pipeline fuses the final norm into the output projection"
    out = _outproj(ya, ym, p, x2d, w_ba.astype(BF16), w_bm.astype(BF16), w_out.astype(BF16), fw,
                   tm=512)
    return out.reshape(batch, seq, d)


def kernel(x, norm_w, w_in, b_in, attn_sinks, conv_w, conv_b, mlstm_norm_w, w_branch_attn,
           w_branch_mlstm, w_out, final_norm_w):
    depth = norm_w.shape[0]
    assert depth == 1
    return _layer(x, norm_w[0], w_in[0], b_in[0], attn_sinks[0], conv_w[0], conv_b[0],
                  mlstm_norm_w[0], w_branch_attn[0], w_branch_mlstm[0], w_out[0], final_norm_w,
                  final=True)
```

```python
import functools

import jax
import jax.numpy as jnp
from jax import lax
from jax.experimental import pallas as pl
from jax.experimental.pallas import tpu as pltpu

F32 = jnp.float32
BF16 = jnp.bfloat16

D_MODEL = 1024
ATTN_HEADS = 16
ATTN_KV_HEADS = 4
ATTN_GROUPS = ATTN_HEADS // ATTN_KV_HEADS
ATTN_HEAD_DIM = 64
ATTN_BLOCK = 128
ATTN_AUG_ROWS = 16
MLSTM_HEADS = 8
MLSTM_QK_DIM = 64
MLSTM_V_DIM = 128
MLSTM_CHUNK = 128
CONV_WIDTH = 4
CONV_HALO = 8
M_INIT = -1e30
RMS_EPS = 1e-6

_SRC = dict(a_q=(0, 1024), a_k=(1024, 256), a_v=(1280, 256), a_z=(1536, 1024),
            m_q=(2560, 512), m_k=(3072, 512), m_v=(3584, 1024), gates=(4608, 16),
            m_o=(4624, 1024), m_z=(5648, 1024), g_a=(6672, 1024), g_m=(7696, 1024))
_ORDER = ("a_q", "a_z", "m_v", "m_o", "m_z", "g_a", "g_m", "m_q", "m_k", "a_k", "a_v")
_OFF = {}
_o = 0
for _name in _ORDER:
    _OFF[_name] = _o
    _o += _SRC[_name][1]
N_MAIN = _o
GATE_PAD = 128
N_CHUNK = 512

V7X_VMEM_BYTES = 64 * 1024 * 1024


def _blk(name):
    off, width = _OFF[name], _SRC[name][1]
    assert off % width == 0
    return off // width


def _vmem_limit(estimate_bytes):
    return int(min(V7X_VMEM_BYTES - (4 << 20), estimate_bytes + (8 << 20)))


def _sigmoid(x):
    return jax.nn.sigmoid(x)


def _silu(x):
    return x * jax.nn.sigmoid(x)


def _inproj_kernel(x_ref, nw_ref, w_ref, b_ref, wg_ref, bg_ref, p_ref, g_ref):
    x = x_ref[...]
    ms = jnp.mean(x * x, axis=-1, keepdims=True)
    h = (x * lax.rsqrt(ms + RMS_EPS) * nw_ref[...]).astype(BF16)
    for c in range(0, N_MAIN, N_CHUNK):
        acc = jnp.dot(h, w_ref[:, c:c + N_CHUNK], preferred_element_type=F32)
        p_ref[:, c:c + N_CHUNK] = (acc + b_ref[:, c:c + N_CHUNK]).astype(BF16)
    g_ref[...] = jnp.dot(h, wg_ref[...], preferred_element_type=F32) + bg_ref[...]


def _inproj(x2d, norm_w, w_main, b_main, w_gate, b_gate, *, tm):
    m = x2d.shape[0]
    est = (2 * tm * D_MODEL * 4 + D_MODEL * N_MAIN * 2 + 2 * tm * N_MAIN * 2
           + 2 * D_MODEL * GATE_PAD * 2 + 2 * tm * GATE_PAD * 4 + tm * N_CHUNK * 4 * 4)
    const = lambda i: (0, 0)
    return pl.pallas_call(
        _inproj_kernel,
        grid=(m // tm,),
        in_specs=[
            pl.BlockSpec((tm, D_MODEL), lambda i: (i, 0)),
            pl.BlockSpec((1, D_MODEL), const),
            pl.BlockSpec((D_MODEL, N_MAIN), const, pipeline_mode=pl.Buffered(1)),
            pl.BlockSpec((1, N_MAIN), const),
            pl.BlockSpec((D_MODEL, GATE_PAD), const),
            pl.BlockSpec((1, GATE_PAD), const),
        ],
        out_specs=[
            pl.BlockSpec((tm, N_MAIN), lambda i: (i, 0)),
            pl.BlockSpec((tm, GATE_PAD), lambda i: (i, 0)),
        ],
        out_shape=[
            jax.ShapeDtypeStruct((m, N_MAIN), BF16),
            jax.ShapeDtypeStruct((m, GATE_PAD), F32),
        ],
        compiler_params=pltpu.CompilerParams(
            dimension_semantics=("parallel",), vmem_limit_bytes=_vmem_limit(est)),
        name="inproj",
    )(x2d, norm_w, w_main, b_main, w_gate, b_gate)


def _attn_kernel(sink_ref, q_ref, kc_ref, kp_ref, vc_ref, vp_ref, o_ref, *, tq):
    n = pl.program_id(1)
    blk = ATTN_BLOCK
    hd = ATTN_HEAD_DIM
    gq = ATTN_GROUPS * blk
    key = lax.broadcasted_iota(jnp.int32, (2 * blk, gq), 0)
    qry = lax.broadcasted_iota(jnp.int32, (2 * blk, gq), 1) & (blk - 1)
    in_window = ((key < blk) & (key > qry)) | ((key >= blk) & ((key - blk) <= qry))
    first_key = jnp.where(n > 0, 0, blk)
    ones_rows = (lax.broadcasted_iota(jnp.int32, (ATTN_AUG_ROWS, 2 * blk), 0) == 0).astype(BF16)
    for s in range(tq // blk):
        rows = slice(s * blk, (s + 1) * blk)
        if s == 0:
            k_prev, v_prev = kp_ref[...], vp_ref[...]
            allowed = in_window & (key >= first_key)
        else:
            prev_rows = slice((s - 1) * blk, s * blk)
            k_prev, v_prev = kc_ref[prev_rows, :], vc_ref[prev_rows, :]
            allowed = in_window
        keys = jnp.concatenate([k_prev, kc_ref[rows, :]], axis=0)
        vals = jnp.concatenate([v_prev, vc_ref[rows, :]], axis=0)
        vals_t = vals.astype(F32).T.astype(BF16)
        for kv in range(ATTN_KV_HEADS):
            heads = [kv * ATTN_GROUPS + j for j in range(ATTN_GROUPS)]
            q_g = jnp.concatenate(
                [q_ref[rows, h * hd:(h + 1) * hd] for h in heads], axis=0)
            sink = sink_ref[kv:kv + 1, :]
            st = lax.dot_general(keys[:, kv * hd:(kv + 1) * hd], q_g,
                                 (((1,), (1,)), ((), ())), preferred_element_type=F32)
            st = jnp.where(allowed, st, -jnp.inf)
            m = jnp.maximum(jnp.max(st, axis=0, keepdims=True), sink)
            p = jnp.exp(st - m).astype(BF16)
            v_aug = jnp.concatenate([vals_t[kv * hd:(kv + 1) * hd, :], ones_rows], axis=0)
            o_aug = jnp.dot(v_aug, p, preferred_element_type=F32)
            den = o_aug[hd:hd + 1, :] + jnp.exp(sink - m)
            o_t = o_aug[0:hd, :] / den
            pairs = [jnp.concatenate([o_t[:, (2 * j) * blk:(2 * j + 1) * blk],
                                      o_t[:, (2 * j + 1) * blk:(2 * j + 2) * blk]], axis=0).T
                     for j in range(ATTN_GROUPS // 2)]
            cols = slice(kv * ATTN_GROUPS * hd, (kv + 1) * ATTN_GROUPS * hd)
            o_ref[rows, cols] = jnp.concatenate(pairs, axis=1).astype(BF16)


def _attn(p, sinks, *, batch, seq, tq):
    m = batch * seq
    nt = seq // tq
    sub = tq // ATTN_BLOCK
    kvw = ATTN_KV_HEADS * ATTN_HEAD_DIM
    cur = lambda b, n: b * nt + n
    prev = lambda b, n: jnp.maximum((b * nt + n) * sub - 1, 0)
    est = 2 * (tq * D_MODEL * 2 + 2 * tq * kvw * 2 + 2 * ATTN_BLOCK * kvw * 2
               + tq * D_MODEL * 2) + (8 << 20)
    sink_rows = jnp.repeat(sinks.astype(F32).reshape(ATTN_KV_HEADS, ATTN_GROUPS), ATTN_BLOCK, axis=1)
    return pl.pallas_call(
        functools.partial(_attn_kernel, tq=tq),
        grid=(batch, nt),
        in_specs=[
            pl.BlockSpec((ATTN_KV_HEADS, ATTN_GROUPS * ATTN_BLOCK), lambda b, n: (0, 0)),
            pl.BlockSpec((tq, D_MODEL), lambda b, n: (cur(b, n), _blk("a_q"))),
            pl.BlockSpec((tq, kvw), lambda b, n: (cur(b, n), _blk("a_k"))),
            pl.BlockSpec((ATTN_BLOCK, kvw), lambda b, n: (prev(b, n), _blk("a_k"))),
            pl.BlockSpec((tq, kvw), lambda b, n: (cur(b, n), _blk("a_v"))),
            pl.BlockSpec((ATTN_BLOCK, kvw), lambda b, n: (prev(b, n), _blk("a_v"))),
        ],
        out_specs=pl.BlockSpec((tq, D_MODEL), lambda b, n: (cur(b, n), 0)),
        out_shape=jax.ShapeDtypeStruct((m, D_MODEL), BF16),
        compiler_params=pltpu.CompilerParams(
            dimension_semantics=("parallel", "arbitrary"), vmem_limit_bytes=_vmem_limit(est)),
        name="swa_attn",
    )(sink_rows, p, p, p, p, p)


def _log_sigmoid(x):
    return jnp.minimum(x, 0.0) - jnp.log1p(jnp.exp(-jnp.abs(x)))


def _mlstm_kernel(q_ref, k_ref, v_ref, g_ref, o_ref, z_ref, cw_ref, cb_ref, nw_ref, y_ref,
                  ubuf, c_st, n_st, m_st):
    L = MLSTM_CHUNK
    H = MLSTM_HEADS
    dk = MLSTM_QK_DIM
    dv = MLSTM_V_DIM
    qkw = H * dk

    @pl.when(pl.program_id(1) == 0)
    def _():
        ubuf[0:CONV_HALO, :] = jnp.zeros((CONV_HALO, 2 * qkw), F32)
        c_st[...] = jnp.zeros(c_st.shape, F32)
        n_st[...] = jnp.zeros(n_st.shape, F32)
        m_st[...] = jnp.full(m_st.shape, M_INIT, F32)

    ubuf[CONV_HALO:CONV_HALO + L, 0:qkw] = q_ref[...].astype(F32)
    ubuf[CONV_HALO:CONV_HALO + L, qkw:2 * qkw] = k_ref[...].astype(F32)
    first = CONV_HALO - (CONV_WIDTH - 1)
    conv = cw_ref[0:1, :] * ubuf[first:first + L, :]
    for j in range(1, CONV_WIDTH):
        conv = conv + cw_ref[j:j + 1, :] * ubuf[first + j:first + j + L, :]
    qk = _silu(conv + cb_ref[...])
    ubuf[0:CONV_HALO, :] = ubuf[L:L + CONV_HALO, :]

    g_t = g_ref[...].T
    i_row = g_t[0:H, :]
    logf = _log_sigmoid(g_t[H:2 * H, :])
    lane = lax.broadcasted_iota(jnp.int32, (H, L), 1)
    b_row = logf
    sh = 1
    while sh < L:
        b_row = b_row + jnp.where(lane >= sh, pltpu.roll(b_row, sh, 1), 0.0)
        sh *= 2
    g_tot = b_row[:, L - 1:L]
    m_prev = m_st[:, 0:1]
    a_row = g_tot - b_row + i_row
    m_new = jnp.maximum(g_tot + m_prev, jnp.max(a_row, axis=-1, keepdims=True))
    s_prev = jnp.exp(g_tot + m_prev - m_new)
    w_row = jnp.exp(a_row - m_new)
    m_st[...] = jnp.broadcast_to(m_new, m_st.shape)
    stacked = jnp.concatenate(
        [b_row, w_row, jnp.zeros((GATE_PAD - 2 * H, L), F32)], axis=0)
    cols = stacked.T

    t_idx = lax.broadcasted_iota(jnp.int32, (L, L), 0)
    s_idx = lax.broadcasted_iota(jnp.int32, (L, L), 1)
    causal = s_idx <= t_idx

    for h in range(H):
        q_h = qk[:, h * dk:(h + 1) * dk] * (dk ** -0.5)
        k_h = qk[:, qkw + h * dk:qkw + (h + 1) * dk]
        v_h = v_ref[:, h * dv:(h + 1) * dv]
        q_b = q_h.astype(BF16)
        b_col = cols[:, h:h + 1]
        w_col = cols[:, H + h:H + h + 1]
        log_d = jnp.where(causal, b_col - b_row[h:h + 1, :] + i_row[h:h + 1, :], -jnp.inf)
        m_inter = b_col + m_prev[h:h + 1, :]
        m_row = jnp.maximum(m_inter, jnp.max(log_d, axis=-1, keepdims=True))
        qk_t = lax.dot_general(q_b, k_h.astype(BF16), (((1,), (1,)), ((), ())),
                               preferred_element_type=F32)
        s_mat = qk_t * jnp.exp(log_d - m_row)
        inter = jnp.exp(m_inter - m_row)
        c_h = c_st[h]
        n_h = n_st[h:h + 1, :]
        num = (jnp.dot(s_mat.astype(BF16), v_h, preferred_element_type=F32)
               + inter * jnp.dot(q_b, c_h.astype(BF16), preferred_element_type=F32))
        den = (jnp.sum(s_mat, axis=-1, keepdims=True)
               + inter * jnp.sum(q_h * n_h, axis=-1, keepdims=True))
        h_t = num / jnp.maximum(jnp.abs(den), jnp.exp(-m_row))

        kw = k_h * w_col
        c_st[h] = s_prev[h:h + 1, :] * c_h + lax.dot_general(
            kw.astype(BF16), v_h, (((0,), (0,)), ((), ())), preferred_element_type=F32)
        n_st[h:h + 1, :] = s_prev[h:h + 1, :] * n_h + jnp.sum(kw, axis=0, keepdims=True)

        cols_h = slice(h * dv, (h + 1) * dv)
        ms = jnp.mean(h_t * h_t, axis=-1, keepdims=True)
        y = h_t * lax.rsqrt(ms + RMS_EPS) * nw_ref[:, cols_h]
        og = o_ref[:, cols_h].astype(F32)
        zg = z_ref[:, cols_h].astype(F32)
        y_ref[:, cols_h] = (_sigmoid(og) * y * _silu(zg)).astype(BF16)


def _mlstm(p, gates, conv_w, conv_b, norm_w, *, batch, seq):
    m = batch * seq
    L = MLSTM_CHUNK
    nt = seq // L
    qkw = MLSTM_HEADS * MLSTM_QK_DIM
    vw = MLSTM_HEADS * MLSTM_V_DIM
    cur = lambda b, n: b * nt + n
    const = lambda b, n: (0, 0)
    est = 2 * (2 * L * qkw * 2 + 3 * L * vw * 2 + L * GATE_PAD * 4 + L * vw * 2) + (16 << 20)
    return pl.pallas_call(
        _mlstm_kernel,
        grid=(batch, nt),
        in_specs=[
            pl.BlockSpec((L, qkw), lambda b, n: (cur(b, n), _blk("m_q"))),
            pl.BlockSpec((L, qkw), lambda b, n: (cur(b, n), _blk("m_k"))),
            pl.BlockSpec((L, vw), lambda b, n: (cur(b, n), _blk("m_v"))),
            pl.BlockSpec((L, GATE_PAD), lambda b, n: (cur(b, n), 0)),
            pl.BlockSpec((L, vw), lambda b, n: (cur(b, n), _blk("m_o"))),
            pl.BlockSpec((L, vw), lambda b, n: (cur(b, n), _blk("m_z"))),
            pl.BlockSpec((CONV_WIDTH, 2 * qkw), const),
            pl.BlockSpec((1, 2 * qkw), const),
            pl.BlockSpec((1, vw), const),
        ],
        out_specs=pl.BlockSpec((L, vw), lambda b, n: (cur(b, n), 0)),
        out_shape=jax.ShapeDtypeStruct((m, vw), BF16),
        scratch_shapes=[
            pltpu.VMEM((CONV_HALO + L, 2 * qkw), F32),
            pltpu.VMEM((MLSTM_HEADS, MLSTM_QK_DIM, MLSTM_V_DIM), F32),
            pltpu.VMEM((MLSTM_HEADS, MLSTM_QK_DIM), F32),
            pltpu.VMEM((MLSTM_HEADS, 128), F32),
        ],
        compiler_params=pltpu.CompilerParams(
            dimension_semantics=("parallel", "arbitrary"), vmem_limit_bytes=_vmem_limit(est)),
        name="mlstm",
    )(p, p, p, gates, p, p, conv_w, conv_b, norm_w)


def _outproj_kernel(ya_ref, az_ref, ym_ref, ga_ref, gm_ref, x_ref, wa_ref, wm_ref, wo_ref, fw_ref,
                    out_ref):
    ya = (ya_ref[...].astype(F32) * _silu(az_ref[...].astype(F32))).astype(BF16)
    br_a = jnp.dot(ya, wa_ref[...], preferred_element_type=F32)
    br_m = jnp.dot(ym_ref[...], wm_ref[...], preferred_element_type=F32)
    merged = (_sigmoid(ga_ref[...].astype(F32)) * br_a
              + _sigmoid(gm_ref[...].astype(F32)) * br_m)
    r = x_ref[...] + jnp.dot(merged.astype(BF16), wo_ref[...], preferred_element_type=F32)
    ms = jnp.mean(r * r, axis=-1, keepdims=True)
    out_ref[...] = r * lax.rsqrt(ms + RMS_EPS) * fw_ref[...]


def _outproj(ya, ym, p, x2d, wa, wm, wo, fw, *, tm):
    m = x2d.shape[0]
    row = lambda i: (i, 0)
    const = lambda i: (0, 0)
    est = (2 * (4 * tm * D_MODEL * 2 + 2 * tm * D_MODEL * 4 + 3 * D_MODEL * D_MODEL * 2)
           + 4 * tm * D_MODEL * 4)
    return pl.pallas_call(
        _outproj_kernel,
        grid=(m // tm,),
        in_specs=[
            pl.BlockSpec((tm, D_MODEL), row),
            pl.BlockSpec((tm, D_MODEL), lambda i: (i, _blk("a_z"))),
            pl.BlockSpec((tm, D_MODEL), row),
            pl.BlockSpec((tm, D_MODEL), lambda i: (i, _blk("g_a"))),
            pl.BlockSpec((tm, D_MODEL), lambda i: (i, _blk("g_m"))),
            pl.BlockSpec((tm, D_MODEL), row),
            pl.BlockSpec((D_MODEL, D_MODEL), const),
            pl.BlockSpec((D_MODEL, D_MODEL), const),
            pl.BlockSpec((D_MODEL, D_MODEL), const),
            pl.BlockSpec((1, D_MODEL), const),
        ],
        out_specs=pl.BlockSpec((tm, D_MODEL), row),
        out_shape=jax.ShapeDtypeStruct((m, D_MODEL), F32),
        compiler_params=pltpu.CompilerParams(
            dimension_semantics=("parallel",), vmem_limit_bytes=_vmem_limit(est)),
        name="outproj",
    )(ya, p, ym, p, p, x2d, wa, wm, wo, fw)


def _layer(x, norm_w, w_in, b_in, sinks, conv_w, conv_b, mnorm_w, w_ba, w_bm, w_out, final_w,
           *, final):
    batch, seq, d = x.shape
    assert d == D_MODEL and seq % 512 == 0
    assert w_in.shape == (D_MODEL, _SRC["g_m"][0] + _SRC["g_m"][1])
    m = batch * seq
    x2d = x.reshape(m, d)

    def seg(a, name):
        off, width = _SRC[name]
        return a[..., off:off + width]

    assert ATTN_HEAD_DIM == 64
    col_scale = {"a_q": ATTN_HEAD_DIM ** -0.5}
    w_main = jnp.concatenate(
        [seg(w_in, nm) * col_scale.get(nm, 1.0) for nm in _ORDER], axis=-1).astype(BF16)
    b_main = jnp.concatenate(
        [seg(b_in, nm) * col_scale.get(nm, 1.0) for nm in _ORDER], axis=-1)[None, :]
    pad = GATE_PAD - _SRC["gates"][1]
    w_gate = jnp.pad(seg(w_in, "gates"), ((0, 0), (0, pad))).astype(BF16)
    b_gate = jnp.pad(seg(b_in, "gates"), ((0, pad),))[None, :]

    p, gates = _inproj(x2d, norm_w[None, :], w_main, b_main, w_gate, b_gate, tm=512)
    ya = _attn(p, sinks, batch=batch, seq=seq, tq=256)
    ym = _mlstm(p, gates, conv_w, conv_b[None, :], mnorm_w[None, :], batch=batch, seq=seq)
    fw = final_w[None, :] if final else None
    assert final, "single-layer pipeline fuses the final norm into the output projection"
    out = _outproj(ya, ym, p, x2d, w_ba.astype(BF16), w_bm.astype(BF16), w_out.astype(BF16), fw,
                   tm=512)
    return out.reshape(batch, seq, d)


def kernel(x, norm_w, w_in, b_in, attn_sinks, conv_w, conv_b, mlstm_norm_w, w_branch_attn,
           w_branch_mlstm, w_out, final_norm_w):
    depth = norm_w.shape[0]
    assert depth == 1
    return _layer(x, norm_w[0], w_in[0], b_in[0], attn_sinks[0], conv_w[0], conv_b[0],
                  mlstm_norm_w[0], w_branch_attn[0], w_branch_mlstm[0], w_out[0], final_norm_w,
                  final=True)
```

```python
import functools

import jax
import jax.numpy as jnp
from jax import lax
from jax.experimental import pallas as pl
from jax.experimental.pallas import tpu as pltpu

F32 = jnp.float32
BF16 = jnp.bfloat16

D_MODEL = 1024
ATTN_HEADS = 16
ATTN_KV_HEADS = 4
ATTN_GROUPS = ATTN_HEADS // ATTN_KV_HEADS
ATTN_HEAD_DIM = 64
ATTN_BLOCK = 128
ATTN_AUG_ROWS = 16
MLSTM_HEADS = 8
MLSTM_QK_DIM = 64
MLSTM_V_DIM = 128
MLSTM_CHUNK = 128
MLSTM_AUG_ROWS = 16
CONV_WIDTH = 4
CONV_HALO = 8
M_INIT = -1e30
RMS_EPS = 1e-6
LANES = 128

_SRC = dict(a_q=(0, 1024), a_k=(1024, 256), a_v=(1280, 256), a_z=(1536, 1024),
            m_q=(2560, 512), m_k=(3072, 512), m_v=(3584, 1024), m_i=(4608, 8), m_f=(4616, 8),
            m_o=(4624, 1024), m_z=(5648, 1024), g_a=(6672, 1024), g_m=(7696, 1024))
_ORDER = ("a_q", "a_z", "m_v", "m_o", "m_z", "g_a", "g_m", "m_q", "m_k", "a_k", "a_v")
_OFF = {}
_o = 0
for _name in _ORDER:
    _OFF[_name] = _o
    _o += _SRC[_name][1]
N_MAIN = _o
GATE_W = 2 * LANES
N_CHUNK = 512

V7X_VMEM_BYTES = 64 * 1024 * 1024


def _blk(name):
    off, width = _OFF[name], _SRC[name][1]
    assert off % width == 0
    return off // width


def _vmem_limit(estimate_bytes):
    return int(min(V7X_VMEM_BYTES - (4 << 20), estimate_bytes + (8 << 20)))


def _sigmoid(x):
    return jax.nn.sigmoid(x)


def _silu(x):
    return x * jax.nn.sigmoid(x)


def _inproj_kernel(x_ref, nw_ref, w_ref, b_ref, wg_ref, bg_ref, p_ref, g_ref):
    x = x_ref[...]
    ms = jnp.mean(x * x, axis=-1, keepdims=True)
    h = (x * lax.rsqrt(ms + RMS_EPS) * nw_ref[...]).astype(BF16)
    for c in range(0, N_MAIN, N_CHUNK):
        acc = jnp.dot(h, w_ref[:, c:c + N_CHUNK], preferred_element_type=F32)
        p_ref[:, c:c + N_CHUNK] = (acc + b_ref[:, c:c + N_CHUNK]).astype(BF16)
    g_ref[...] = jnp.dot(h, wg_ref[...], preferred_element_type=F32) + bg_ref[...]


def _inproj(x2d, norm_w, w_main, b_main, w_gate, b_gate, *, tm):
    m = x2d.shape[0]
    est = (2 * tm * D_MODEL * 4 + D_MODEL * N_MAIN * 2 + 2 * tm * N_MAIN * 2
           + 2 * D_MODEL * GATE_W * 2 + 2 * tm * GATE_W * 4 + tm * N_CHUNK * 4 * 4)
    const = lambda i: (0, 0)
    return pl.pallas_call(
        _inproj_kernel,
        grid=(m // tm,),
        in_specs=[
            pl.BlockSpec((tm, D_MODEL), lambda i: (i, 0)),
            pl.BlockSpec((1, D_MODEL), const),
            pl.BlockSpec((D_MODEL, N_MAIN), const, pipeline_mode=pl.Buffered(1)),
            pl.BlockSpec((1, N_MAIN), const),
            pl.BlockSpec((D_MODEL, GATE_W), const),
            pl.BlockSpec((1, GATE_W), const),
        ],
        out_specs=[
            pl.BlockSpec((tm, N_MAIN), lambda i: (i, 0)),
            pl.BlockSpec((tm, GATE_W), lambda i: (i, 0)),
        ],
        out_shape=[
            jax.ShapeDtypeStruct((m, N_MAIN), BF16),
            jax.ShapeDtypeStruct((m, GATE_W), F32),
        ],
        compiler_params=pltpu.CompilerParams(
            dimension_semantics=("parallel",), vmem_limit_bytes=_vmem_limit(est)),
        name="inproj",
    )(x2d, norm_w, w_main, b_main, w_gate, b_gate)


def _attn_kernel(sink_ref, q_ref, kc_ref, kp_ref, vc_ref, vp_ref, o_ref, *, tq):
    n = pl.program_id(1)
    blk = ATTN_BLOCK
    hd = ATTN_HEAD_DIM
    gq = ATTN_GROUPS * blk
    key = lax.broadcasted_iota(jnp.int32, (2 * blk, gq), 0)
    qry = lax.broadcasted_iota(jnp.int32, (2 * blk, gq), 1) & (blk - 1)
    in_window = ((key < blk) & (key > qry)) | ((key >= blk) & ((key - blk) <= qry))
    first_key = jnp.where(n > 0, 0, blk)
    ones_rows = (lax.broadcasted_iota(jnp.int32, (ATTN_AUG_ROWS, 2 * blk), 0) == 0).astype(BF16)
    for s in range(tq // blk):
        rows = slice(s * blk, (s + 1) * blk)
        if s == 0:
            k_prev, v_prev = kp_ref[...], vp_ref[...]
            allowed = in_window & (key >= first_key)
        else:
            prev_rows = slice((s - 1) * blk, s * blk)
            k_prev, v_prev = kc_ref[prev_rows, :], vc_ref[prev_rows, :]
            allowed = in_window
        keys = jnp.concatenate([k_prev, kc_ref[rows, :]], axis=0)
        vals = jnp.concatenate([v_prev, vc_ref[rows, :]], axis=0)
        vals_t = vals.astype(F32).T.astype(BF16)
        for kv in range(ATTN_KV_HEADS):
            heads = [kv * ATTN_GROUPS + j for j in range(ATTN_GROUPS)]
            q_g = jnp.concatenate(
                [q_ref[rows, h * hd:(h + 1) * hd] for h in heads], axis=0)
            sink = sink_ref[kv:kv + 1, :]
            st = lax.dot_general(keys[:, kv * hd:(kv + 1) * hd], q_g,
                                 (((1,), (1,)), ((), ())), preferred_element_type=F32)
            st = jnp.where(allowed, st, -jnp.inf)
            m = jnp.maximum(jnp.max(st, axis=0, keepdims=True), sink)
            p = jnp.exp(st - m).astype(BF16)
            v_aug = jnp.concatenate([vals_t[kv * hd:(kv + 1) * hd, :], ones_rows], axis=0)
            o_aug = jnp.dot(v_aug, p, preferred_element_type=F32)
            den = o_aug[hd:hd + 1, :] + jnp.exp(sink - m)
            o_t = o_aug[0:hd, :] / den
            pairs = [jnp.concatenate([o_t[:, (2 * j) * blk:(2 * j + 1) * blk],
                                      o_t[:, (2 * j + 1) * blk:(2 * j + 2) * blk]], axis=0).T
                     for j in range(ATTN_GROUPS // 2)]
            cols = slice(kv * ATTN_GROUPS * hd, (kv + 1) * ATTN_GROUPS * hd)
            o_ref[rows, cols] = jnp.concatenate(pairs, axis=1).astype(BF16)


def _attn(p, sinks, *, batch, seq, tq):
    m = batch * seq
    nt = seq // tq
    sub = tq // ATTN_BLOCK
    kvw = ATTN_KV_HEADS * ATTN_HEAD_DIM
    cur = lambda b, n: b * nt + n
    prev = lambda b, n: jnp.maximum((b * nt + n) * sub - 1, 0)
    est = 2 * (tq * D_MODEL * 2 + 2 * tq * kvw * 2 + 2 * ATTN_BLOCK * kvw * 2
               + tq * D_MODEL * 2) + (8 << 20)
    sink_rows = jnp.repeat(sinks.astype(F32).reshape(ATTN_KV_HEADS, ATTN_GROUPS), ATTN_BLOCK, axis=1)
    return pl.pallas_call(
        functools.partial(_attn_kernel, tq=tq),
        grid=(batch, nt),
        in_specs=[
            pl.BlockSpec((ATTN_KV_HEADS, ATTN_GROUPS * ATTN_BLOCK), lambda b, n: (0, 0)),
            pl.BlockSpec((tq, D_MODEL), lambda b, n: (cur(b, n), _blk("a_q"))),
            pl.BlockSpec((tq, kvw), lambda b, n: (cur(b, n), _blk("a_k"))),
            pl.BlockSpec((ATTN_BLOCK, kvw), lambda b, n: (prev(b, n), _blk("a_k"))),
            pl.BlockSpec((tq, kvw), lambda b, n: (cur(b, n), _blk("a_v"))),
            pl.BlockSpec((ATTN_BLOCK, kvw), lambda b, n: (prev(b, n), _blk("a_v"))),
        ],
        out_specs=pl.BlockSpec((tq, D_MODEL), lambda b, n: (cur(b, n), 0)),
        out_shape=jax.ShapeDtypeStruct((m, D_MODEL), BF16),
        compiler_params=pltpu.CompilerParams(
            dimension_semantics=("parallel", "arbitrary"), vmem_limit_bytes=_vmem_limit(est)),
        name="swa_attn",
    )(sink_rows, p, p, p, p, p)


def _log_sigmoid(x):
    return jnp.minimum(x, 0.0) - jnp.log1p(jnp.exp(-jnp.abs(x)))


def _scan_time(x, combine, fill):
    t = lax.broadcasted_iota(jnp.int32, x.shape, 0)
    sh = 1
    while sh < x.shape[0]:
        x = combine(x, jnp.where(t >= sh, pltpu.roll(x, sh, 0), fill))
        sh *= 2
    return x


def _mlstm_kernel(q_ref, k_ref, v_ref, g_ref, cw_ref, cb_ref, nw_ref, y_ref,
                  ubuf, c_st, m_st, *, n_chunks):
    L = MLSTM_CHUNK
    H = MLSTM_HEADS
    dk = MLSTM_QK_DIM
    dv = MLSTM_V_DIM
    aug = MLSTM_AUG_ROWS
    qkw = H * dk

    @pl.when(pl.program_id(1) == 0)
    def _():
        ubuf[0:CONV_HALO, :] = jnp.zeros((CONV_HALO, 2 * qkw), F32)
        c_st[...] = jnp.zeros(c_st.shape, F32)
        m_st[...] = jnp.full(m_st.shape, M_INIT, F32)

    s_idx = lax.broadcasted_iota(jnp.int32, (L, L), 0)
    t_idx = lax.broadcasted_iota(jnp.int32, (L, L), 1)
    causal_t = s_idx <= t_idx
    group = lax.broadcasted_iota(jnp.int32, (L, LANES), 1) // H
    ones_rows = (lax.broadcasted_iota(jnp.int32, (aug, L), 0) == 0).astype(F32)
    m_prev = m_st[...]

    for c in range(n_chunks):
        rows = slice(c * L, (c + 1) * L)
        lo = CONV_HALO + c * L
        ubuf[lo:lo + L, 0:qkw] = q_ref[rows, :].astype(F32)
        ubuf[lo:lo + L, qkw:2 * qkw] = k_ref[rows, :].astype(F32)
        first = lo - (CONV_WIDTH - 1)
        conv = cw_ref[0:1, :] * ubuf[first:first + L, :]
        for j in range(1, CONV_WIDTH):
            conv = conv + cw_ref[j:j + 1, :] * ubuf[first + j:first + j + L, :]
        qk = _silu(conv + cb_ref[...])
        q_b = (qk[:, 0:qkw] * (dk ** -0.5)).astype(BF16)
        k_b = qk[:, qkw:2 * qkw].astype(BF16)

        i_g = g_ref[rows, 0:LANES]
        b = _scan_time(_log_sigmoid(g_ref[rows, LANES:2 * LANES]), jnp.add, 0.0)
        g_tot = b[L - 1:L, :]
        a = g_tot - b + i_g
        m_new = jnp.maximum(g_tot + m_prev, jnp.max(a, axis=0, keepdims=True))
        s_prev = jnp.exp(g_tot + m_prev - m_new)
        w = jnp.exp(a - m_new)
        r = i_g - b
        m_inter = b + m_prev
        m_row = jnp.maximum(m_inter, b + _scan_time(r, jnp.maximum, -jnp.inf))
        packed = jnp.where(group == 0, b - m_row,
                 jnp.where(group == 1, jnp.exp(m_inter - m_row),
                 jnp.where(group == 2, jnp.exp(-m_row),
                 jnp.where(group == 3, w, jnp.broadcast_to(s_prev, (L, LANES))))))
        lane_rows = packed.T
        m_prev = m_new

        for h in range(H):
            dm_row = lane_rows[h:h + 1, :]
            inter_row = lane_rows[H + h:H + h + 1, :]
            floor_row = lane_rows[2 * H + h:2 * H + h + 1, :]
            w_row = lane_rows[3 * H + h:3 * H + h + 1, :]
            sp_row = lane_rows[4 * H + h:4 * H + h + 1, 0:dk]
            q_h = q_b[:, h * dk:(h + 1) * dk]
            k_h = k_b[:, h * dk:(h + 1) * dk]
            v_t = v_ref[rows, h * dv:(h + 1) * dv].astype(F32).T
            c_h = c_st[h]

            prod = lax.dot_general(
                jnp.concatenate([k_h, c_h.astype(BF16)], axis=0), q_h,
                (((1,), (1,)), ((), ())), preferred_element_type=F32)
            decay_t = jnp.exp(jnp.where(
                causal_t, jnp.broadcast_to(r[:, h:h + 1], (L, L)) + dm_row, -jnp.inf))
            s_t = (prod[0:L, :] * decay_t).astype(BF16)
            v_aug = jnp.concatenate([v_t, ones_rows], axis=0)
            num = (jnp.dot(v_aug.astype(BF16), s_t, preferred_element_type=F32)
                   + inter_row * prod[L:L + dv + aug, :])
            inv = 1.0 / jnp.maximum(jnp.abs(num[dv:dv + 1, :]), floor_row)
            h_t = num[0:dv, :] * inv

            c_st[h] = sp_row * c_h + jnp.dot(
                (v_aug * w_row).astype(BF16), k_h, preferred_element_type=F32)

            ms = jnp.mean(h_t * h_t, axis=0, keepdims=True)
            y_t = h_t * lax.rsqrt(ms + RMS_EPS) * nw_ref[h * dv:(h + 1) * dv, :]
            y_ref[rows, h * dv:(h + 1) * dv] = y_t.T.astype(BF16)

    m_st[...] = m_prev
    top = CONV_HALO + n_chunks * L
    ubuf[0:CONV_HALO, :] = ubuf[top - CONV_HALO:top, :]


def _mlstm(p, gates, conv_w, conv_b, norm_w, *, batch, seq, n_chunks):
    m = batch * seq
    L = MLSTM_CHUNK
    tb = n_chunks * L
    nt = seq // tb
    qkw = MLSTM_HEADS * MLSTM_QK_DIM
    vw = MLSTM_HEADS * MLSTM_V_DIM
    cur = lambda b, n: b * nt + n
    const = lambda b, n: (0, 0)
    nw_cols = jnp.broadcast_to(norm_w.astype(F32)[:, None], (vw, L))
    est = (2 * (2 * tb * qkw * 2 + tb * vw * 2 + tb * GATE_W * 4 + tb * vw * 2)
           + 2 * vw * L * 4 + (CONV_HALO + tb) * 2 * qkw * 4 + (16 << 20))
    return pl.pallas_call(
        functools.partial(_mlstm_kernel, n_chunks=n_chunks),
        grid=(batch, nt),
        in_specs=[
            pl.BlockSpec((tb, qkw), lambda b, n: (cur(b, n), _blk("m_q"))),
            pl.BlockSpec((tb, qkw), lambda b, n: (cur(b, n), _blk("m_k"))),
            pl.BlockSpec((tb, vw), lambda b, n: (cur(b, n), _blk("m_v"))),
            pl.BlockSpec((tb, GATE_W), lambda b, n: (cur(b, n), 0)),
            pl.BlockSpec((CONV_WIDTH, 2 * qkw), const),
            pl.BlockSpec((1, 2 * qkw), const),
            pl.BlockSpec((vw, L), const),
        ],
        out_specs=pl.BlockSpec((tb, vw), lambda b, n: (cur(b, n), 0)),
        out_shape=jax.ShapeDtypeStruct((m, vw), BF16),
        scratch_shapes=[
            pltpu.VMEM((CONV_HALO + tb, 2 * qkw), F32),
            pltpu.VMEM((MLSTM_HEADS, MLSTM_V_DIM + MLSTM_AUG_ROWS, MLSTM_QK_DIM), F32),
            pltpu.VMEM((1, LANES), F32),
        ],
        compiler_params=pltpu.CompilerParams(
            dimension_semantics=("parallel", "arbitrary"), vmem_limit_bytes=_vmem_limit(est)),
        name="mlstm",
    )(p, p, p, gates, conv_w, conv_b, nw_cols)


def _outproj_kernel(ya_ref, az_ref, ym_ref, mo_ref, mz_ref, ga_ref, gm_ref, x_ref,
                    wa_ref, wm_ref, wo_ref, fw_ref, out_ref):
    ya = (ya_ref[...].astype(F32) * _silu(az_ref[...].astype(F32))).astype(BF16)
    br_a = jnp.dot(ya, wa_ref[...], preferred_element_type=F32)
    ym = (_sigmoid(mo_ref[...].astype(F32)) * ym_ref[...].astype(F32)
          * _silu(mz_ref[...].astype(F32))).astype(BF16)
    br_m = jnp.dot(ym, wm_ref[...], preferred_element_type=F32)
    merged = (_sigmoid(ga_ref[...].astype(F32)) * br_a
              + _sigmoid(gm_ref[...].astype(F32)) * br_m)
    r = x_ref[...] + jnp.dot(merged.astype(BF16), wo_ref[...], preferred_element_type=F32)
    ms = jnp.mean(r * r, axis=-1, keepdims=True)
    out_ref[...] = r * lax.rsqrt(ms + RMS_EPS) * fw_ref[...]


def _outproj(ya, ym, p, x2d, wa, wm, wo, fw, *, tm):
    m = x2d.shape[0]
    row = lambda i: (i, 0)
    const = lambda i: (0, 0)
    seg = lambda name: pl.BlockSpec((tm, D_MODEL), lambda i: (i, _blk(name)))
    est = (2 * (7 * tm * D_MODEL * 2 + 2 * tm * D_MODEL * 4 + 3 * D_MODEL * D_MODEL * 2)
           + 6 * tm * D_MODEL * 4)
    return pl.pallas_call(
        _outproj_kernel,
        grid=(m // tm,),
        in_specs=[
            pl.BlockSpec((tm, D_MODEL), row), seg("a_z"),
            pl.BlockSpec((tm, D_MODEL), row), seg("m_o"), seg("m_z"),
            seg("g_a"), seg("g_m"),
            pl.BlockSpec((tm, D_MODEL), row),
            pl.BlockSpec((D_MODEL, D_MODEL), const),
            pl.BlockSpec((D_MODEL, D_MODEL), const),
            pl.BlockSpec((D_MODEL, D_MODEL), const),
            pl.BlockSpec((1, D_MODEL), const),
        ],
        out_specs=pl.BlockSpec((tm, D_MODEL), row),
        out_shape=jax.ShapeDtypeStruct((m, D_MODEL), F32),
        compiler_params=pltpu.CompilerParams(
            dimension_semantics=("parallel",), vmem_limit_bytes=_vmem_limit(est)),
        name="outproj",
    )(ya, p, ym, p, p, p, p, x2d, wa, wm, wo, fw)


def _layer(x, norm_w, w_in, b_in, sinks, conv_w, conv_b, mnorm_w, w_ba, w_bm, w_out, final_w):
    batch, seq, d = x.shape
    assert d == D_MODEL and seq % 512 == 0
    assert w_in.shape == (D_MODEL, _SRC["g_m"][0] + _SRC["g_m"][1])
    m = batch * seq
    x2d = x.reshape(m, d)

    def seg(a, name):
        off, width = _SRC[name]
        return a[..., off:off + width]

    assert ATTN_HEAD_DIM == 64
    col_scale = {"a_q": ATTN_HEAD_DIM ** -0.5}
    w_main = jnp.concatenate(
        [seg(w_in, nm) * col_scale.get(nm, 1.0) for nm in _ORDER], axis=-1).astype(BF16)
    b_main = jnp.concatenate(
        [seg(b_in, nm) * col_scale.get(nm, 1.0) for nm in _ORDER], axis=-1)[None, :]
    rep = LANES // MLSTM_HEADS
    w_gate = jnp.concatenate(
        [jnp.tile(seg(w_in, "m_i"), (1, rep)), jnp.tile(seg(w_in, "m_f"), (1, rep))],
        axis=-1).astype(BF16)
    b_gate = jnp.concatenate(
        [jnp.tile(seg(b_in, "m_i"), rep), jnp.tile(seg(b_in, "m_f"), rep)], axis=-1)[None, :]

    p, gates = _inproj(x2d, norm_w[None, :], w_main, b_main, w_gate, b_gate, tm=512)
    ya = _attn(p, sinks, batch=batch, seq=seq, tq=256)
    ym = _mlstm(p, gates, conv_w, conv_b[None, :], mnorm_w, batch=batch, seq=seq, n_chunks=2)
    out = _outproj(ya, ym, p, x2d, w_ba.astype(BF16), w_bm.astype(BF16), w_out.astype(BF16),
                   final_w[None, :], tm=512)
    return out.reshape(batch, seq, d)


def kernel(x, norm_w, w_in, b_in, attn_sinks, conv_w, conv_b, mlstm_norm_w, w_branch_attn,
           w_branch_mlstm, w_out, final_norm_w):
    assert norm_w.shape[0] == 1, "single-layer pipeline: the final norm is fused into the layer"
    return _layer(x, norm_w[0], w_in[0], b_in[0], attn_sinks[0], conv_w[0], conv_b[0],
                  mlstm_norm_w[0], w_branch_attn[0], w_branch_mlstm[0], w_out[0], final_norm_w)
```

```python
import functools

import jax
import jax.numpy as jnp
from jax import lax
from jax.experimental import pallas as pl
from jax.experimental.pallas import tpu as pltpu

F32 = jnp.float32
BF16 = jnp.bfloat16

D_MODEL = 1024
ATTN_HEADS = 16
ATTN_KV_HEADS = 4
ATTN_GROUPS = ATTN_HEADS // ATTN_KV_HEADS
ATTN_HEAD_DIM = 64
ATTN_BLOCK = 128
ATTN_AUG_ROWS = 16
MLSTM_HEADS = 8
MLSTM_QK_DIM = 64
MLSTM_V_DIM = 128
MLSTM_CHUNK = 128
MLSTM_AUG_ROWS = 16
MLSTM_LOOKAHEAD = 3
ATTN_LOOKAHEAD = 2
CONV_WIDTH = 4
CONV_HALO = 8
M_INIT = -1e30
RMS_EPS = 1e-6
LANES = 128
LOG2_E = 1.4426950408889634

_SRC = dict(a_q=(0, 1024), a_k=(1024, 256), a_v=(1280, 256), a_z=(1536, 1024),
            m_q=(2560, 512), m_k=(3072, 512), m_v=(3584, 1024), m_i=(4608, 8), m_f=(4616, 8),
            m_o=(4624, 1024), m_z=(5648, 1024), g_a=(6672, 1024), g_m=(7696, 1024))
_ORDER = ("a_q", "a_z", "m_v", "m_o", "m_z", "g_a", "g_m", "m_q", "m_k", "a_k", "a_v")
_ACT = dict(a_z="silu", m_o="sigmoid", m_z="silu", g_a="sigmoid", g_m="sigmoid",
            m_q="conv", m_k="conv")
_OFF = {}
_o = 0
for _name in _ORDER:
    _OFF[_name] = _o
    _o += _SRC[_name][1]
N_MAIN = _o
N_CHUNK = 512
GATE_W = 2 * LANES
SCAN_W = 3 * LANES

V7X_VMEM_BYTES = 64 * 1024 * 1024


def _blk(name):
    off, width = _OFF[name], _SRC[name][1]
    assert off % width == 0
    return off // width


def _vmem_limit(estimate_bytes):
    return int(min(V7X_VMEM_BYTES - (4 << 20), estimate_bytes + (8 << 20)))


def _sigmoid(x):
    return 0.5 * jnp.tanh(0.5 * x) + 0.5


def _silu(x):
    return x * _sigmoid(x)


def _log_sigmoid(x):
    return jnp.minimum(x, 0.0) - jnp.log1p(jnp.exp(-jnp.abs(x)))


def _scan_time(x, combine, fill):
    t = lax.broadcasted_iota(jnp.int32, x.shape, 0)
    sh = 1
    while sh < x.shape[0]:
        x = combine(x, jnp.where(t >= sh, pltpu.roll(x, sh, 0), fill))
        sh *= 2
    return x


def _inproj_kernel(x_ref, nw_ref, w_ref, b_ref, wg_ref, bg_ref, cw_ref, cb_ref,
                   p_ref, s_ref, ubuf, *, tiles_per_seq):
    tm = x_ref.shape[0]
    L = MLSTM_CHUNK
    qkw = MLSTM_HEADS * MLSTM_QK_DIM

    @pl.when(pl.program_id(0) % tiles_per_seq == 0)
    def _():
        ubuf[0:CONV_HALO, :] = jnp.zeros((CONV_HALO, 2 * qkw), F32)

    x = x_ref[...]
    ms = jnp.mean(x * x, axis=-1, keepdims=True)
    h = (x * lax.rsqrt(ms + RMS_EPS) * nw_ref[...]).astype(BF16)

    def project(c, cw):
        return jnp.dot(h, w_ref[:, c:c + cw], preferred_element_type=F32) + b_ref[:, c:c + cw]

    for name in ("m_q", "m_k"):
        u0 = _OFF[name] - _OFF["m_q"]
        ubuf[CONV_HALO:CONV_HALO + tm, u0:u0 + qkw] = project(_OFF[name], qkw)
    g = jnp.dot(h, wg_ref[...], preferred_element_type=F32) + bg_ref[...]

    first = CONV_HALO - (CONV_WIDTH - 1)

    def conv_half(u0):
        conv = cw_ref[0:1, u0:u0 + N_CHUNK] * ubuf[first:first + tm, u0:u0 + N_CHUNK]
        for j in range(1, CONV_WIDTH):
            conv = conv + (cw_ref[j:j + 1, u0:u0 + N_CHUNK]
                           * ubuf[first + j:first + j + tm, u0:u0 + N_CHUNK])
        qk = _silu(conv + cb_ref[:, u0:u0 + N_CHUNK])
        if u0 < qkw:
            qk = qk * (MLSTM_QK_DIM ** -0.5)
        c = _OFF["m_q"] + u0
        p_ref[:, c:c + N_CHUNK] = qk.astype(BF16)

    def gate_scans(t0):
        i_g = g[t0:t0 + L, 0:LANES]
        b = _scan_time(_log_sigmoid(g[t0:t0 + L, LANES:2 * LANES]), jnp.add, 0.0)
        r = i_g - b
        s_ref[t0:t0 + L, 0:LANES] = b
        s_ref[t0:t0 + L, LANES:2 * LANES] = r
        s_ref[t0:t0 + L, 2 * LANES:3 * LANES] = _scan_time(r, jnp.maximum, -jnp.inf)

    def plain_chunk(name, c):
        act = _ACT.get(name)
        acc = project(c, N_CHUNK)
        if act == "silu":
            acc = _silu(acc)
        elif act == "sigmoid":
            acc = _sigmoid(acc)
        p_ref[:, c:c + N_CHUNK] = acc.astype(BF16)

    vpu_work = ([functools.partial(conv_half, u0) for u0 in range(0, 2 * qkw, N_CHUNK)]
                + [functools.partial(gate_scans, t0) for t0 in range(0, tm, L)])
    chunks = []
    for name in _ORDER:
        off, width = _OFF[name], _SRC[name][1]
        if _ACT.get(name) == "conv" or name == "a_v":
            continue
        assert width % N_CHUNK == 0 or name == "a_k"
        chunks += [(name, c) for c in range(off, off + width, N_CHUNK)]
    per_group = 2
    while chunks:
        group, chunks = chunks[:per_group], chunks[per_group:]
        for name, c in group:
            plain_chunk(name, c)
        if vpu_work:
            vpu_work.pop(0)()
    assert not vpu_work
    ubuf[0:CONV_HALO, :] = ubuf[tm:tm + CONV_HALO, :]


def _inproj(x2d, norm_w, w_main, b_main, w_gate, b_gate, conv_w, conv_b, *, tm, seq):
    m = x2d.shape[0]
    qkw2 = 2 * MLSTM_HEADS * MLSTM_QK_DIM
    est = (2 * tm * D_MODEL * 4 + D_MODEL * N_MAIN * 2 + 2 * tm * N_MAIN * 2
           + 2 * D_MODEL * GATE_W * 2 + 2 * tm * SCAN_W * 4 + (CONV_HALO + tm) * qkw2 * 4
           + 6 * tm * N_CHUNK * 4)
    const = lambda i: (0, 0)
    return pl.pallas_call(
        functools.partial(_inproj_kernel, tiles_per_seq=seq // tm),
        grid=(m // tm,),
        in_specs=[
            pl.BlockSpec((tm, D_MODEL), lambda i: (i, 0)),
            pl.BlockSpec((1, D_MODEL), const),
            pl.BlockSpec((D_MODEL, N_MAIN), const, pipeline_mode=pl.Buffered(1)),
            pl.BlockSpec((1, N_MAIN), const),
            pl.BlockSpec((D_MODEL, GATE_W), const),
            pl.BlockSpec((1, GATE_W), const),
            pl.BlockSpec((CONV_WIDTH, qkw2), const),
            pl.BlockSpec((1, qkw2), const),
        ],
        out_specs=[
            pl.BlockSpec((tm, N_MAIN), lambda i: (i, 0)),
            pl.BlockSpec((tm, SCAN_W), lambda i: (i, 0)),
        ],
        out_shape=[
            jax.ShapeDtypeStruct((m, N_MAIN), BF16),
            jax.ShapeDtypeStruct((m, SCAN_W), F32),
        ],
        scratch_shapes=[pltpu.VMEM((CONV_HALO + tm, qkw2), F32)],
        compiler_params=pltpu.CompilerParams(
            dimension_semantics=("arbitrary",), vmem_limit_bytes=_vmem_limit(est)),
        name="inproj",
    )(x2d, norm_w, w_main, b_main, w_gate, b_gate, conv_w, conv_b)


def _attn_kernel(sink_ref, bias_ref, q_ref, kc_ref, kp_ref, vc_ref, vp_ref, o_ref, st_ring,
                 *, tq):
    n = pl.program_id(1)
    blk = ATTN_BLOCK
    hd = ATTN_HEAD_DIM
    ones_rows = (lax.broadcasted_iota(jnp.int32, (ATTN_AUG_ROWS, 2 * blk), 0) == 0).astype(BF16)
    lane_half = lax.broadcasted_iota(jnp.int32, (1, 2 * hd), 1) // hd
    blocks = []
    for s in range(tq // blk):
        rows = slice(s * blk, (s + 1) * blk)
        if s == 0:
            k_prev, v_prev = kp_ref[...], vp_ref[...]
            bias = bias_ref[jnp.where(n > 0, 0, 1)]
        else:
            prev_rows = slice((s - 1) * blk, s * blk)
            k_prev, v_prev = kc_ref[prev_rows, :], vc_ref[prev_rows, :]
            bias = bias_ref[0]
        keys = jnp.concatenate([k_prev, kc_ref[rows, :]], axis=0)
        vals = jnp.concatenate([v_prev, vc_ref[rows, :]], axis=0)
        vals_t = vals.astype(F32).T.astype(BF16)
        k_sel = [jnp.where(lane_half == (kv % 2),
                           keys[:, (kv // 2) * 2 * hd:(kv // 2 + 1) * 2 * hd], jnp.zeros((), BF16))
                 for kv in range(ATTN_KV_HEADS)]
        v_aug = [jnp.concatenate([vals_t[kv * hd:(kv + 1) * hd, :], ones_rows], axis=0)
                 for kv in range(ATTN_KV_HEADS)]
        blocks.append((rows, bias, k_sel, v_aug))

    jobs = [(s, kv) for s in range(tq // blk) for kv in range(ATTN_KV_HEADS)]

    def scores(s, kv):
        rows, _, k_sel, _ = blocks[s]
        tiles = [(kv // 2) * ATTN_GROUPS + j for j in range(ATTN_GROUPS)]
        q_g = jnp.concatenate(
            [q_ref[rows, t * 2 * hd:(t + 1) * 2 * hd] for t in tiles], axis=0)
        slot = jobs.index((s, kv)) % (ATTN_LOOKAHEAD + 1)
        st_ring[slot] = lax.dot_general(k_sel[kv], q_g, (((1,), (1,)), ((), ())),
                                        preferred_element_type=F32)
        return slot

    def finish(s, kv, slot):
        rows, bias, _, v_aug = blocks[s]
        sink = sink_ref[kv:kv + 1, :]
        st = st_ring[slot] + bias
        m = jnp.maximum(jnp.max(st, axis=0, keepdims=True), sink)
        p = jnp.exp2(st - m).astype(BF16)
        o_aug = jnp.dot(v_aug[kv], p, preferred_element_type=F32)
        den = o_aug[hd:hd + 1, :] + jnp.exp2(sink - m)
        o_t = o_aug[0:hd, :] / den
        pairs = [jnp.concatenate([o_t[:, (2 * j) * blk:(2 * j + 1) * blk],
                                  o_t[:, (2 * j + 1) * blk:(2 * j + 2) * blk]], axis=0).T
                 for j in range(ATTN_GROUPS // 2)]
        cols = slice(kv * ATTN_GROUPS * hd, (kv + 1) * ATTN_GROUPS * hd)
        o_ref[rows, cols] = jnp.concatenate(pairs, axis=1).astype(BF16)

    started = {}
    for idx, job in enumerate(jobs):
        for ahead in jobs[idx:idx + ATTN_LOOKAHEAD + 1]:
            if ahead not in started:
                started[ahead] = scores(*ahead)
        finish(*job, started.pop(job))


def _attn_bias():
    blk = ATTN_BLOCK
    key = jnp.arange(2 * blk)[:, None]
    qry = (jnp.arange(ATTN_GROUPS * blk) % blk)[None, :]
    in_window = ((key < blk) & (key > qry)) | ((key >= blk) & ((key - blk) <= qry))
    first = in_window & (key >= blk)
    return jnp.where(jnp.stack([in_window, first]), 0.0, -jnp.inf).astype(F32)


def _attn(p, sinks, *, batch, seq, tq):
    m = batch * seq
    nt = seq // tq
    sub = tq // ATTN_BLOCK
    kvw = ATTN_KV_HEADS * ATTN_HEAD_DIM
    gq = ATTN_GROUPS * ATTN_BLOCK
    cur = lambda b, n: b * nt + n
    prev = lambda b, n: jnp.maximum((b * nt + n) * sub - 1, 0)
    est = 2 * (tq * D_MODEL * 2 + 2 * tq * kvw * 2 + 2 * ATTN_BLOCK * kvw * 2
               + tq * D_MODEL * 2 + 2 * 2 * ATTN_BLOCK * gq * 4) + (8 << 20)
    sink_rows = jnp.repeat(
        (sinks.astype(F32) * LOG2_E).reshape(ATTN_KV_HEADS, ATTN_GROUPS), ATTN_BLOCK, axis=1)
    return pl.pallas_call(
        functools.partial(_attn_kernel, tq=tq),
        grid=(batch, nt),
        in_specs=[
            pl.BlockSpec((ATTN_KV_HEADS, gq), lambda b, n: (0, 0)),
            pl.BlockSpec((2, 2 * ATTN_BLOCK, gq), lambda b, n: (0, 0, 0)),
            pl.BlockSpec((tq, D_MODEL), lambda b, n: (cur(b, n), _blk("a_q"))),
            pl.BlockSpec((tq, kvw), lambda b, n: (cur(b, n), _blk("a_k"))),
            pl.BlockSpec((ATTN_BLOCK, kvw), lambda b, n: (prev(b, n), _blk("a_k"))),
            pl.BlockSpec((tq, kvw), lambda b, n: (cur(b, n), _blk("a_v"))),
            pl.BlockSpec((ATTN_BLOCK, kvw), lambda b, n: (prev(b, n), _blk("a_v"))),
        ],
        out_specs=pl.BlockSpec((tq, D_MODEL), lambda b, n: (cur(b, n), 0)),
        out_shape=jax.ShapeDtypeStruct((m, D_MODEL), BF16),
        scratch_shapes=[pltpu.VMEM((ATTN_LOOKAHEAD + 1, 2 * ATTN_BLOCK, gq), F32)],
        compiler_params=pltpu.CompilerParams(
            dimension_semantics=("parallel", "arbitrary"), vmem_limit_bytes=_vmem_limit(est)),
        name="swa_attn",
    )(sink_rows, _attn_bias(), p, p, p, p, p)


def _mlstm_kernel(q_ref, k_ref, v_ref, s_ref, nw_ref, y_ref, c_st, m_st, *, n_chunks):
    L = MLSTM_CHUNK
    H = MLSTM_HEADS
    dk = MLSTM_QK_DIM
    dv = MLSTM_V_DIM
    aug = MLSTM_AUG_ROWS

    @pl.when(pl.program_id(1) == 0)
    def _():
        c_st[...] = jnp.zeros(c_st.shape, F32)
        m_st[...] = jnp.full(m_st.shape, M_INIT, F32)

    s_idx = lax.broadcasted_iota(jnp.int32, (L, L), 0)
    t_idx = lax.broadcasted_iota(jnp.int32, (L, L), 1)
    causal_t = s_idx <= t_idx
    group = lax.broadcasted_iota(jnp.int32, (L, LANES), 1) // H
    ones_rows = (lax.broadcasted_iota(jnp.int32, (aug, L), 0) == 0).astype(F32)
    m_prev = m_st[...]

    gate_rows = []
    for c in range(n_chunks):
        rows = slice(c * L, (c + 1) * L)
        b = s_ref[rows, 0:LANES]
        r = s_ref[rows, LANES:2 * LANES]
        g_tot = b[L - 1:L, :]
        a = g_tot + r
        m_new = jnp.maximum(g_tot + m_prev, jnp.max(a, axis=0, keepdims=True))
        s_prev = jnp.exp(g_tot + m_prev - m_new)
        m_inter = b + m_prev
        m_row = jnp.maximum(m_inter, b + s_ref[rows, 2 * LANES:3 * LANES])
        packed = jnp.where(group == 0, b - m_row,
                 jnp.where(group == 1, jnp.exp(m_inter - m_row),
                 jnp.where(group == 2, jnp.exp(-m_row),
                 jnp.where(group == 3, jnp.exp(a - m_new),
                           jnp.broadcast_to(s_prev, (L, LANES))))))
        gate_rows.append((r, packed.T))
        m_prev = m_new
    m_st[...] = m_prev

    lane_half = lax.broadcasted_iota(jnp.int32, (1, LANES), 1) // dk

    def prepare(c, h):
        rows = slice(c * L, (c + 1) * L)
        pair = slice((h // 2) * 2 * dk, (h // 2 + 1) * 2 * dk)
        mine = lane_half == (h % 2)
        r, lane_rows = gate_rows[c]
        dm_row = lane_rows[h:h + 1, :]
        w_row = lane_rows[3 * H + h:3 * H + h + 1, :]
        q_h = jnp.where(mine, q_ref[rows, pair], jnp.zeros((), BF16))
        k_p = k_ref[rows, pair]
        v_t = v_ref[rows, h * dv:(h + 1) * dv].astype(F32).T
        v_aug = jnp.concatenate([v_t, ones_rows], axis=0)
        decay_t = jnp.exp(jnp.where(
            causal_t, jnp.broadcast_to(r[:, h:h + 1], (L, L)) + dm_row, -jnp.inf))
        return dict(q=q_h, k=k_p, v=v_aug.astype(BF16), vw=(v_aug * w_row).astype(BF16),
                    decay=decay_t, mine=mine)

    def read_out(c, h, job):
        c_h = c_st[h]
        prod = lax.dot_general(
            jnp.concatenate([job["k"], c_h.astype(BF16)], axis=0), job["q"],
            (((1,), (1,)), ((), ())), preferred_element_type=F32)
        return c_h, prod

    def finish(c, h, job, c_h, prod):
        rows = slice(c * L, (c + 1) * L)
        _, lane_rows = gate_rows[c]
        inter_row = lane_rows[H + h:H + h + 1, :]
        floor_row = lane_rows[2 * H + h:2 * H + h + 1, :]
        sp_row = lane_rows[4 * H + h:4 * H + h + 1, :]
        s_t = (prod[0:L, :] * job["decay"]).astype(BF16)
        num = (jnp.dot(job["v"], s_t, preferred_element_type=F32)
               + inter_row * prod[L:L + dv + aug, :])
        inv = 1.0 / jnp.maximum(jnp.abs(num[dv:dv + 1, :]), floor_row)
        h_t = num[0:dv, :] * inv
        upd = jnp.dot(job["vw"], job["k"], preferred_element_type=F32)
        c_st[h] = jnp.where(job["mine"], sp_row * c_h + upd, 0.0)
        ms = jnp.mean(h_t * h_t, axis=0, keepdims=True)
        y_t = h_t * lax.rsqrt(ms + RMS_EPS) * nw_ref[h * dv:(h + 1) * dv, :]
        y_ref[rows, h * dv:(h + 1) * dv] = y_t.T.astype(BF16)

    jobs = [(c, h) for c in range(n_chunks) for h in range(H)]
    assert MLSTM_LOOKAHEAD < H
    prepared = {job: prepare(*job) for job in jobs}
    started = {}
    for idx, job in enumerate(jobs):
        for ahead in jobs[idx:idx + MLSTM_LOOKAHEAD + 1]:
            if ahead not in started:
                started[ahead] = read_out(*ahead, prepared[ahead])
        finish(*job, prepared[job], *started.pop(job))


def _mlstm(p, scans, norm_w, *, batch, seq, n_chunks):
    m = batch * seq
    L = MLSTM_CHUNK
    tb = n_chunks * L
    nt = seq // tb
    qkw = MLSTM_HEADS * MLSTM_QK_DIM
    vw = MLSTM_HEADS * MLSTM_V_DIM
    cur = lambda b, n: b * nt + n
    nw_cols = jnp.broadcast_to(norm_w.astype(F32)[:, None], (vw, L))
    est = (2 * (2 * tb * qkw * 2 + tb * vw * 2 + tb * SCAN_W * 4 + tb * vw * 2)
           + 2 * vw * L * 4 + (16 << 20))
    return pl.pallas_call(
        functools.partial(_mlstm_kernel, n_chunks=n_chunks),
        grid=(batch, nt),
        in_specs=[
            pl.BlockSpec((tb, qkw), lambda b, n: (cur(b, n), _blk("m_q"))),
            pl.BlockSpec((tb, qkw), lambda b, n: (cur(b, n), _blk("m_k"))),
            pl.BlockSpec((tb, vw), lambda b, n: (cur(b, n), _blk("m_v"))),
            pl.BlockSpec((tb, SCAN_W), lambda b, n: (cur(b, n), 0)),
            pl.BlockSpec((vw, L), lambda b, n: (0, 0)),
        ],
        out_specs=pl.BlockSpec((tb, vw), lambda b, n: (cur(b, n), 0)),
        out_shape=jax.ShapeDtypeStruct((m, vw), BF16),
        scratch_shapes=[
            pltpu.VMEM((MLSTM_HEADS, MLSTM_V_DIM + MLSTM_AUG_ROWS, 2 * MLSTM_QK_DIM), F32),
            pltpu.VMEM((1, LANES), F32),
        ],
        compiler_params=pltpu.CompilerParams(
            dimension_semantics=("parallel", "arbitrary"), vmem_limit_bytes=_vmem_limit(est)),
        name="mlstm",
    )(p, p, p, scans, nw_cols)


def _outproj_kernel(ya_ref, za_ref, ym_ref, so_ref, sz_ref, ga_ref, gm_ref, x_ref,
                    wa_ref, wm_ref, wo_ref, fw_ref, out_ref):
    br_a = jnp.dot(ya_ref[...] * za_ref[...], wa_ref[...], preferred_element_type=F32)
    br_m = jnp.dot(so_ref[...] * ym_ref[...] * sz_ref[...], wm_ref[...],
                   preferred_element_type=F32)
    merged = ga_ref[...].astype(F32) * br_a + gm_ref[...].astype(F32) * br_m
    r = x_ref[...] + jnp.dot(merged.astype(BF16), wo_ref[...], preferred_element_type=F32)
    ms = jnp.mean(r * r, axis=-1, keepdims=True)
    out_ref[...] = r * lax.rsqrt(ms + RMS_EPS) * fw_ref[...]


def _outproj(ya, ym, p, x2d, wa, wm, wo, fw, *, tm):
    m = x2d.shape[0]
    row = lambda i: (i, 0)
    const = lambda i: (0, 0)
    seg = lambda name: pl.BlockSpec((tm, D_MODEL), lambda i: (i, _blk(name)))
    est = (2 * (7 * tm * D_MODEL * 2 + 2 * tm * D_MODEL * 4 + 3 * D_MODEL * D_MODEL * 2)
           + 6 * tm * D_MODEL * 4)
    return pl.pallas_call(
        _outproj_kernel,
        grid=(m // tm,),
        in_specs=[
            pl.BlockSpec((tm, D_MODEL), row), seg("a_z"),
            pl.BlockSpec((tm, D_MODEL), row), seg("m_o"), seg("m_z"),
            seg("g_a"), seg("g_m"),
            pl.BlockSpec((tm, D_MODEL), row),
            pl.BlockSpec((D_MODEL, D_MODEL), const),
            pl.BlockSpec((D_MODEL, D_MODEL), const),
            pl.BlockSpec((D_MODEL, D_MODEL), const),
            pl.BlockSpec((1, D_MODEL), const),
        ],
        out_specs=pl.BlockSpec((tm, D_MODEL), row),
        out_shape=jax.ShapeDtypeStruct((m, D_MODEL), F32),
        compiler_params=pltpu.CompilerParams(
            dimension_semantics=("parallel",), vmem_limit_bytes=_vmem_limit(est)),
        name="outproj",
    )(ya, p, ym, p, p, p, p, x2d, wa, wm, wo, fw)


def _layer(x, norm_w, w_in, b_in, sinks, conv_w, conv_b, mnorm_w, w_ba, w_bm, w_out, final_w):
    batch, seq, d = x.shape
    assert d == D_MODEL and seq % 512 == 0
    assert w_in.shape == (D_MODEL, _SRC["g_m"][0] + _SRC["g_m"][1])
    m = batch * seq
    x2d = x.reshape(m, d)

    def seg(a, name):
        off, width = _SRC[name]
        return a[..., off:off + width]

    def main_cols(a):
        def one(nm):
            cols = seg(a, nm)
            if nm == "a_q":
                cols = cols * (ATTN_HEAD_DIM ** -0.5 * LOG2_E)
                hd, g = ATTN_HEAD_DIM, ATTN_GROUPS
                heads = cols.reshape(cols.shape[:-1] + (ATTN_KV_HEADS // 2, 2, g, hd))
                cols = jnp.swapaxes(heads, -3, -2).reshape(cols.shape)
            return cols
        return jnp.concatenate([one(nm) for nm in _ORDER], axis=-1)

    w_main = main_cols(w_in).astype(BF16)
    b_main = main_cols(b_in)[None, :]
    rep = LANES // MLSTM_HEADS
    w_gate = jnp.concatenate(
        [jnp.tile(seg(w_in, "m_i"), (1, rep)), jnp.tile(seg(w_in, "m_f"), (1, rep))],
        axis=-1).astype(BF16)
    b_gate = jnp.concatenate(
        [jnp.tile(seg(b_in, "m_i"), rep), jnp.tile(seg(b_in, "m_f"), rep)], axis=-1)[None, :]

    p, scans = _inproj(x2d, norm_w[None, :], w_main, b_main, w_gate, b_gate, conv_w,
                       conv_b[None, :], tm=512, seq=seq)
    ya = _attn(p, sinks, batch=batch, seq=seq, tq=512)
    ym = _mlstm(p, scans, mnorm_w, batch=batch, seq=seq, n_chunks=4)
    out = _outproj(ya, ym, p, x2d, w_ba.astype(BF16), w_bm.astype(BF16), w_out.astype(BF16),
                   final_w[None, :], tm=512)
    return out.reshape(batch, seq, d)


def kernel(x, norm_w, w_in, b_in, attn_sinks, conv_w, conv_b, mlstm_norm_w, w_branch_attn,
           w_branch_mlstm, w_out, final_norm_w):
    assert norm_w.shape[0] == 1, "single-layer pipeline: the final norm is fused into the layer"
    return _layer(x, norm_w[0], w_in[0], b_in[0], attn_sinks[0], conv_w[0], conv_b[0],
                  mlstm_norm_w[0], w_branch_attn[0], w_branch_mlstm[0], w_out[0], final_norm_w)
```

```python
import functools

import jax
import jax.numpy as jnp
from jax import lax
from jax.experimental import pallas as pl
from jax.experimental.pallas import tpu as pltpu

F32 = jnp.float32
BF16 = jnp.bfloat16

D_MODEL = 1024
ATTN_HEADS = 16
ATTN_KV_HEADS = 4
ATTN_GROUPS = ATTN_HEADS // ATTN_KV_HEADS
ATTN_HEAD_DIM = 64
ATTN_BLOCK = 128
ATTN_AUG_ROWS = 16
MLSTM_HEADS = 8
MLSTM_QK_DIM = 64
MLSTM_V_DIM = 128
MLSTM_CHUNK = 128
MLSTM_AUG_ROWS = 16
MLSTM_LOOKAHEAD = 3
ATTN_LOOKAHEAD = 2
CONV_WIDTH = 4
CONV_HALO = 8
M_INIT = -1e30
RMS_EPS = 1e-6
LANES = 128
LOG2_E = 1.4426950408889634

_SRC = dict(a_q=(0, 1024), a_k=(1024, 256), a_v=(1280, 256), a_z=(1536, 1024),
            m_q=(2560, 512), m_k=(3072, 512), m_v=(3584, 1024), m_i=(4608, 8), m_f=(4616, 8),
            m_o=(4624, 1024), m_z=(5648, 1024), g_a=(6672, 1024), g_m=(7696, 1024))
_ORDER = ("a_q", "a_z", "m_v", "m_o", "m_z", "g_a", "g_m", "m_q", "m_k", "a_k", "a_v")
_ACT = dict(a_z="silu", m_o="sigmoid", m_z="silu", g_a="sigmoid", g_m="sigmoid",
            m_q="conv", m_k="conv")
_OFF = {}
_o = 0
for _name in _ORDER:
    _OFF[_name] = _o
    _o += _SRC[_name][1]
N_MAIN = _o
N_CHUNK = 512
GATE_W = 2 * LANES
SCAN_W = 3 * LANES

V7X_VMEM_BYTES = 64 * 1024 * 1024


def _blk(name):
    off, width = _OFF[name], _SRC[name][1]
    assert off % width == 0
    return off // width


def _vmem_limit(estimate_bytes):
    return int(min(V7X_VMEM_BYTES - (4 << 20), estimate_bytes + (8 << 20)))


def _sigmoid(x):
    return 0.5 * jnp.tanh(0.5 * x) + 0.5


def _silu(x):
    return x * _sigmoid(x)


def _log_sigmoid(x):
    return jnp.minimum(x, 0.0) - jnp.log1p(jnp.exp(-jnp.abs(x)))


def _scan_time(x, combine, fill):
    t = lax.broadcasted_iota(jnp.int32, x.shape, 0)
    sh = 1
    while sh < x.shape[0]:
        x = combine(x, jnp.where(t >= sh, pltpu.roll(x, sh, 0), fill))
        sh *= 2
    return x


def _prep_kernel(w_ref, sel_ref, wm_ref, wg_ref):
    hd, g = ATTN_HEAD_DIM, ATTN_GROUPS
    for name in _ORDER:
        src, width = _SRC[name]
        dst = _OFF[name]
        if name == "a_q":
            scale = hd ** -0.5 * LOG2_E
            for t in range(ATTN_HEADS // 2):
                lo = (t // g) * 2 * g + t % g
                for half, head in enumerate((lo, lo + g)):
                    wm_ref[:, dst + (2 * t + half) * hd:dst + (2 * t + half + 1) * hd] = (
                        w_ref[:, src + head * hd:src + (head + 1) * hd] * scale).astype(BF16)
        else:
            wm_ref[:, dst:dst + width] = w_ref[:, src:src + width].astype(BF16)
    gi = _SRC["m_i"][0]
    gates = w_ref[:, gi:gi + 2 * MLSTM_HEADS].astype(BF16)
    wg_ref[...] = jnp.dot(gates, sel_ref[...], preferred_element_type=F32).astype(BF16)


def _gate_select():
    j = jnp.arange(2 * MLSTM_HEADS)[:, None]
    lane = jnp.arange(GATE_W)[None, :]
    return ((lane // LANES) * MLSTM_HEADS + lane % MLSTM_HEADS == j).astype(BF16)


def _prep_weights(w_in, *, rows):
    d, n_src = w_in.shape
    return pl.pallas_call(
        _prep_kernel,
        grid=(d // rows,),
        in_specs=[pl.BlockSpec((rows, n_src), lambda i: (i, 0)),
                  pl.BlockSpec((2 * MLSTM_HEADS, GATE_W), lambda i: (0, 0))],
        out_specs=[pl.BlockSpec((rows, N_MAIN), lambda i: (i, 0)),
                   pl.BlockSpec((rows, GATE_W), lambda i: (i, 0))],
        out_shape=[jax.ShapeDtypeStruct((d, N_MAIN), BF16),
                   jax.ShapeDtypeStruct((d, GATE_W), BF16)],
        compiler_params=pltpu.CompilerParams(
            dimension_semantics=("parallel",),
            vmem_limit_bytes=_vmem_limit(2 * rows * (n_src * 4 + N_MAIN * 2 + GATE_W * 2))),
        name="prep_weights",
    )(w_in, _gate_select())


def _inproj_kernel(x_ref, nw_ref, w_ref, b_ref, wg_ref, bg_ref, cw_ref, cb_ref,
                   p_ref, s_ref, halo, *, tiles_per_seq):
    tm = x_ref.shape[0]
    L = MLSTM_CHUNK
    qkw = MLSTM_HEADS * MLSTM_QK_DIM

    @pl.when(pl.program_id(0) % tiles_per_seq == 0)
    def _():
        halo[...] = jnp.zeros(halo.shape, F32)

    x = x_ref[...]
    ms = jnp.mean(x * x, axis=-1, keepdims=True)
    h = (x * lax.rsqrt(ms + RMS_EPS) * nw_ref[...]).astype(BF16)

    def project(c, cw):
        return jnp.dot(h, w_ref[:, c:c + cw], preferred_element_type=F32) + b_ref[:, c:c + cw]

    u_halves = [project(_OFF[name], qkw) for name in ("m_q", "m_k")]
    g = jnp.dot(h, wg_ref[...], preferred_element_type=F32) + bg_ref[...]

    def shift_time(a, carry_in):
        rolled = pltpu.roll(a, 1, 0)
        row = lax.broadcasted_iota(jnp.int32, (CONV_HALO, a.shape[1]), 0)
        head = jnp.where(row == 0, carry_in, rolled[0:CONV_HALO, :])
        return jnp.concatenate([head, rolled[CONV_HALO:, :]], axis=0)

    def conv_half(u0):
        u = u_halves[u0 // qkw]
        cols = slice(u0, u0 + N_CHUNK)
        w = [cw_ref[j:j + 1, cols] for j in range(CONV_WIDTH)]
        prev = [halo[CONV_HALO - 1 - d:CONV_HALO - d, cols] for d in range(CONV_WIDTH - 1)]
        acc = w[0] * u
        carry = w[0] * prev[0]
        for j in range(1, CONV_WIDTH):
            acc = w[j] * u + shift_time(acc, carry)
            if j < CONV_WIDTH - 1:
                carry = sum(w[j - d] * prev[d] for d in range(j + 1))
        halo[:, cols] = u[tm - CONV_HALO:tm, :]
        qk = _silu(acc + cb_ref[:, cols])
        if u0 < qkw:
            qk = qk * (MLSTM_QK_DIM ** -0.5)
        c = _OFF["m_q"] + u0
        p_ref[:, c:c + N_CHUNK] = qk.astype(BF16)

    def gate_scans(t0):
        i_g = g[t0:t0 + L, 0:LANES]
        b = _scan_time(_log_sigmoid(g[t0:t0 + L, LANES:2 * LANES]), jnp.add, 0.0)
        r = i_g - b
        s_ref[t0:t0 + L, 0:LANES] = b
        s_ref[t0:t0 + L, LANES:2 * LANES] = r
        s_ref[t0:t0 + L, 2 * LANES:3 * LANES] = _scan_time(r, jnp.maximum, -jnp.inf)

    def plain_chunk(name, c):
        act = _ACT.get(name)
        acc = project(c, N_CHUNK)
        if act == "silu":
            acc = _silu(acc)
        elif act == "sigmoid":
            acc = _sigmoid(acc)
        p_ref[:, c:c + N_CHUNK] = acc.astype(BF16)

    vpu_work = ([functools.partial(conv_half, u0) for u0 in range(0, 2 * qkw, N_CHUNK)]
                + [functools.partial(gate_scans, t0) for t0 in range(0, tm, L)])
    chunks = []
    for name in _ORDER:
        off, width = _OFF[name], _SRC[name][1]
        if _ACT.get(name) == "conv" or name == "a_v":
            continue
        assert width % N_CHUNK == 0 or name == "a_k"
        chunks += [(name, c) for c in range(off, off + width, N_CHUNK)]
    per_group = 2
    while chunks:
        group, chunks = chunks[:per_group], chunks[per_group:]
        for name, c in group:
            plain_chunk(name, c)
        if vpu_work:
            vpu_work.pop(0)()
    assert not vpu_work


def _inproj(x2d, norm_w, w_main, b_main, w_gate, b_gate, conv_w, conv_b, *, tm, seq):
    m = x2d.shape[0]
    qkw2 = 2 * MLSTM_HEADS * MLSTM_QK_DIM
    est = (2 * tm * D_MODEL * 4 + D_MODEL * N_MAIN * 2 + 2 * tm * N_MAIN * 2
           + 2 * D_MODEL * GATE_W * 2 + 2 * tm * SCAN_W * 4 + (CONV_HALO + tm) * qkw2 * 4
           + 6 * tm * N_CHUNK * 4)
    const = lambda i: (0, 0)
    return pl.pallas_call(
        functools.partial(_inproj_kernel, tiles_per_seq=seq // tm),
        grid=(m // tm,),
        in_specs=[
            pl.BlockSpec((tm, D_MODEL), lambda i: (i, 0)),
            pl.BlockSpec((1, D_MODEL), const),
            pl.BlockSpec((D_MODEL, N_MAIN), const, pipeline_mode=pl.Buffered(1)),
            pl.BlockSpec((1, N_MAIN), const),
            pl.BlockSpec((D_MODEL, GATE_W), const),
            pl.BlockSpec((1, GATE_W), const),
            pl.BlockSpec((CONV_WIDTH, qkw2), const),
            pl.BlockSpec((1, qkw2), const),
        ],
        out_specs=[
            pl.BlockSpec((tm, N_MAIN), lambda i: (i, 0)),
            pl.BlockSpec((tm, SCAN_W), lambda i: (i, 0)),
        ],
        out_shape=[
            jax.ShapeDtypeStruct((m, N_MAIN), BF16),
            jax.ShapeDtypeStruct((m, SCAN_W), F32),
        ],
        scratch_shapes=[pltpu.VMEM((CONV_HALO, qkw2), F32)],
        compiler_params=pltpu.CompilerParams(
            dimension_semantics=("arbitrary",), vmem_limit_bytes=_vmem_limit(est)),
        name="inproj",
    )(x2d, norm_w, w_main, b_main, w_gate, b_gate, conv_w, conv_b)


def _attn_kernel(sink_ref, bias_ref, q_ref, kc_ref, kp_ref, vc_ref, vp_ref, o_ref, st_ring,
                 *, tq):
    n = pl.program_id(1)
    blk = ATTN_BLOCK
    hd = ATTN_HEAD_DIM
    ones_rows = (lax.broadcasted_iota(jnp.int32, (ATTN_AUG_ROWS, 2 * blk), 0) == 0).astype(BF16)
    lane_half = lax.broadcasted_iota(jnp.int32, (1, 2 * hd), 1) // hd
    blocks = []
    for s in range(tq // blk):
        rows = slice(s * blk, (s + 1) * blk)
        if s == 0:
            k_prev, v_prev = kp_ref[...], vp_ref[...]
            bias = bias_ref[jnp.where(n > 0, 0, 1)]
        else:
            prev_rows = slice((s - 1) * blk, s * blk)
            k_prev, v_prev = kc_ref[prev_rows, :], vc_ref[prev_rows, :]
            bias = bias_ref[0]
        keys = jnp.concatenate([k_prev, kc_ref[rows, :]], axis=0)
        vals = jnp.concatenate([v_prev, vc_ref[rows, :]], axis=0)
        vals_t = vals.astype(F32).T.astype(BF16)
        k_sel = [jnp.where(lane_half == (kv % 2),
                           keys[:, (kv // 2) * 2 * hd:(kv // 2 + 1) * 2 * hd], jnp.zeros((), BF16))
                 for kv in range(ATTN_KV_HEADS)]
        v_aug = [jnp.concatenate([vals_t[kv * hd:(kv + 1) * hd, :], ones_rows], axis=0)
                 for kv in range(ATTN_KV_HEADS)]
        blocks.append((rows, bias, k_sel, v_aug))

    jobs = [(s, kv) for s in range(tq // blk) for kv in range(ATTN_KV_HEADS)]

    def scores(s, kv):
        rows, _, k_sel, _ = blocks[s]
        tiles = [(kv // 2) * ATTN_GROUPS + j for j in range(ATTN_GROUPS)]
        q_g = jnp.concatenate(
            [q_ref[rows, t * 2 * hd:(t + 1) * 2 * hd] for t in tiles], axis=0)
        slot = jobs.index((s, kv)) % (ATTN_LOOKAHEAD + 1)
        st_ring[slot] = lax.dot_general(k_sel[kv], q_g, (((1,), (1,)), ((), ())),
                                        preferred_element_type=F32)
        return slot

    def finish(s, kv, slot):
        rows, bias, _, v_aug = blocks[s]
        sink = sink_ref[kv:kv + 1, :]
        st = st_ring[slot] + bias
        m = jnp.maximum(jnp.max(st, axis=0, keepdims=True), sink)
        p = jnp.exp2(st - m).astype(BF16)
        o_aug = jnp.dot(v_aug[kv], p, preferred_element_type=F32)
        den = o_aug[hd:hd + 1, :] + jnp.exp2(sink - m)
        o_t = o_aug[0:hd, :] / den
        pairs = [jnp.concatenate([o_t[:, (2 * j) * blk:(2 * j + 1) * blk],
                                  o_t[:, (2 * j + 1) * blk:(2 * j + 2) * blk]], axis=0).T
                 for j in range(ATTN_GROUPS // 2)]
        cols = slice(kv * ATTN_GROUPS * hd, (kv + 1) * ATTN_GROUPS * hd)
        o_ref[rows, cols] = jnp.concatenate(pairs, axis=1).astype(BF16)

    started = {}
    for idx, job in enumerate(jobs):
        for ahead in jobs[idx:idx + ATTN_LOOKAHEAD + 1]:
            if ahead not in started:
                started[ahead] = scores(*ahead)
        finish(*job, started.pop(job))


def _attn_bias():
    blk = ATTN_BLOCK
    key = jnp.arange(2 * blk)[:, None]
    qry = (jnp.arange(ATTN_GROUPS * blk) % blk)[None, :]
    in_window = ((key < blk) & (key > qry)) | ((key >= blk) & ((key - blk) <= qry))
    first = in_window & (key >= blk)
    return jnp.where(jnp.stack([in_window, first]), 0.0, -jnp.inf).astype(F32)


def _attn(p, sinks, *, batch, seq, tq):
    m = batch * seq
    nt = seq // tq
    sub = tq // ATTN_BLOCK
    kvw = ATTN_KV_HEADS * ATTN_HEAD_DIM
    gq = ATTN_GROUPS * ATTN_BLOCK
    cur = lambda b, n: b * nt + n
    prev = lambda b, n: jnp.maximum((b * nt + n) * sub - 1, 0)
    est = 2 * (tq * D_MODEL * 2 + 2 * tq * kvw * 2 + 2 * ATTN_BLOCK * kvw * 2
               + tq * D_MODEL * 2 + 2 * 2 * ATTN_BLOCK * gq * 4) + (8 << 20)
    sink_rows = jnp.repeat(
        (sinks.astype(F32) * LOG2_E).reshape(ATTN_KV_HEADS, ATTN_GROUPS), ATTN_BLOCK, axis=1)
    return pl.pallas_call(
        functools.partial(_attn_kernel, tq=tq),
        grid=(batch, nt),
        in_specs=[
            pl.BlockSpec((ATTN_KV_HEADS, gq), lambda b, n: (0, 0)),
            pl.BlockSpec((2, 2 * ATTN_BLOCK, gq), lambda b, n: (0, 0, 0)),
            pl.BlockSpec((tq, D_MODEL), lambda b, n: (cur(b, n), _blk("a_q"))),
            pl.BlockSpec((tq, kvw), lambda b, n: (cur(b, n), _blk("a_k"))),
            pl.BlockSpec((ATTN_BLOCK, kvw), lambda b, n: (prev(b, n), _blk("a_k"))),
            pl.BlockSpec((tq, kvw), lambda b, n: (cur(b, n), _blk("a_v"))),
            pl.BlockSpec((ATTN_BLOCK, kvw), lambda b, n: (prev(b, n), _blk("a_v"))),
        ],
        out_specs=pl.BlockSpec((tq, D_MODEL), lambda b, n: (cur(b, n), 0)),
        out_shape=jax.ShapeDtypeStruct((m, D_MODEL), BF16),
        scratch_shapes=[pltpu.VMEM((ATTN_LOOKAHEAD + 1, 2 * ATTN_BLOCK, gq), F32)],
        compiler_params=pltpu.CompilerParams(
            dimension_semantics=("parallel", "arbitrary"), vmem_limit_bytes=_vmem_limit(est)),
        name="swa_attn",
    )(sink_rows, _attn_bias(), p, p, p, p, p)


def _mlstm_kernel(q_ref, k_ref, v_ref, s_ref, nw_ref, y_ref, c_st, m_st, *, n_chunks):
    L = MLSTM_CHUNK
    H = MLSTM_HEADS
    dk = MLSTM_QK_DIM
    dv = MLSTM_V_DIM
    aug = MLSTM_AUG_ROWS

    @pl.when(pl.program_id(1) == 0)
    def _():
        c_st[...] = jnp.zeros(c_st.shape, F32)
        m_st[...] = jnp.full(m_st.shape, M_INIT, F32)

    s_idx = lax.broadcasted_iota(jnp.int32, (L, L), 0)
    t_idx = lax.broadcasted_iota(jnp.int32, (L, L), 1)
    causal_t = s_idx <= t_idx
    group = lax.broadcasted_iota(jnp.int32, (L, LANES), 1) // H
    ones_rows = (lax.broadcasted_iota(jnp.int32, (aug, L), 0) == 0).astype(F32)
    m_prev = m_st[...]

    gate_rows = []
    for c in range(n_chunks):
        rows = slice(c * L, (c + 1) * L)
        b = s_ref[rows, 0:LANES]
        r = s_ref[rows, LANES:2 * LANES]
        g_tot = b[L - 1:L, :]
        a = g_tot + r
        m_new = jnp.maximum(g_tot + m_prev, jnp.max(a, axis=0, keepdims=True))
        s_prev = jnp.exp(g_tot + m_prev - m_new)
        m_inter = b + m_prev
        m_row = jnp.maximum(m_inter, b + s_ref[rows, 2 * LANES:3 * LANES])
        packed = jnp.where(group == 0, b - m_row,
                 jnp.where(group == 1, jnp.exp(m_inter - m_row),
                 jnp.where(group == 2, jnp.exp(-m_row),
                 jnp.where(group == 3, jnp.exp(a - m_new),
                           jnp.broadcast_to(s_prev, (L, LANES))))))
        gate_rows.append((r, packed.T))
        m_prev = m_new
    m_st[...] = m_prev

    lane_half = lax.broadcasted_iota(jnp.int32, (1, LANES), 1) // dk

    def prepare(c, h):
        rows = slice(c * L, (c + 1) * L)
        pair = slice((h // 2) * 2 * dk, (h // 2 + 1) * 2 * dk)
        mine = lane_half == (h % 2)
        r, lane_rows = gate_rows[c]
        dm_row = lane_rows[h:h + 1, :]
        w_row = lane_rows[3 * H + h:3 * H + h + 1, :]
        q_h = jnp.where(mine, q_ref[rows, pair], jnp.zeros((), BF16))
        k_p = k_ref[rows, pair]
        v_t = v_ref[rows, h * dv:(h + 1) * dv].astype(F32).T
        v_aug = jnp.concatenate([v_t, ones_rows], axis=0)
        decay_t = jnp.exp(jnp.where(
            causal_t, jnp.broadcast_to(r[:, h:h + 1], (L, L)) + dm_row, -jnp.inf))
        return dict(q=q_h, k=k_p, v=v_aug.astype(BF16), vw=(v_aug * w_row).astype(BF16),
                    decay=decay_t, mine=mine)

    def read_out(c, h, job):
        c_h = c_st[h]
        prod = lax.dot_general(
            jnp.concatenate([job["k"], c_h.astype(BF16)], axis=0), job["q"],
            (((1,), (1,)), ((), ())), preferred_element_type=F32)
        return c_h, prod

    def finish(c, h, job, c_h, prod):
        rows = slice(c * L, (c + 1) * L)
        _, lane_rows = gate_rows[c]
        inter_row = lane_rows[H + h:H + h + 1, :]
        floor_row = lane_rows[2 * H + h:2 * H + h + 1, :]
        sp_row = lane_rows[4 * H + h:4 * H + h + 1, :]
        s_t = (prod[0:L, :] * job["decay"]).astype(BF16)
        num = (jnp.dot(job["v"], s_t, preferred_element_type=F32)
               + inter_row * prod[L:L + dv + aug, :])
        inv = 1.0 / jnp.maximum(jnp.abs(num[dv:dv + 1, :]), floor_row)
        h_t = num[0:dv, :] * inv
        upd = jnp.dot(job["vw"], job["k"], preferred_element_type=F32)
        c_st[h] = jnp.where(job["mine"], sp_row * c_h + upd, 0.0)
        ms = jnp.mean(h_t * h_t, axis=0, keepdims=True)
        y_t = h_t * lax.rsqrt(ms + RMS_EPS) * nw_ref[h * dv:(h + 1) * dv, :]
        y_ref[rows, h * dv:(h + 1) * dv] = y_t.T.astype(BF16)

    jobs = [(c, h) for c in range(n_chunks) for h in range(H)]
    assert MLSTM_LOOKAHEAD < H
    prepared = {job: prepare(*job) for job in jobs}
    started = {}
    for idx, job in enumerate(jobs):
        for ahead in jobs[idx:idx + MLSTM_LOOKAHEAD + 1]:
            if ahead not in started:
                started[ahead] = read_out(*ahead, prepared[ahead])
        finish(*job, prepared[job], *started.pop(job))


def _mlstm(p, scans, norm_w, *, batch, seq, n_chunks):
    m = batch * seq
    L = MLSTM_CHUNK
    tb = n_chunks * L
    nt = seq // tb
    qkw = MLSTM_HEADS * MLSTM_QK_DIM
    vw = MLSTM_HEADS * MLSTM_V_DIM
    cur = lambda b, n: b * nt + n
    nw_cols = jnp.broadcast_to(norm_w.astype(F32)[:, None], (vw, L))
    est = (2 * (2 * tb * qkw * 2 + tb * vw * 2 + tb * SCAN_W * 4 + tb * vw * 2)
           + 2 * vw * L * 4 + (16 << 20))
    return pl.pallas_call(
        functools.partial(_mlstm_kernel, n_chunks=n_chunks),
        grid=(batch, nt),
        in_specs=[
            pl.BlockSpec((tb, qkw), lambda b, n: (cur(b, n), _blk("m_q"))),
            pl.BlockSpec((tb, qkw), lambda b, n: (cur(b, n), _blk("m_k"))),
            pl.BlockSpec((tb, vw), lambda b, n: (cur(b, n), _blk("m_v"))),
            pl.BlockSpec((tb, SCAN_W), lambda b, n: (cur(b, n), 0)),
            pl.BlockSpec((vw, L), lambda b, n: (0, 0)),
        ],
        out_specs=pl.BlockSpec((tb, vw), lambda b, n: (cur(b, n), 0)),
        out_shape=jax.ShapeDtypeStruct((m, vw), BF16),
        scratch_shapes=[
            pltpu.VMEM((MLSTM_HEADS, MLSTM_V_DIM + MLSTM_AUG_ROWS, 2 * MLSTM_QK_DIM), F32),
            pltpu.VMEM((1, LANES), F32),
        ],
        compiler_params=pltpu.CompilerParams(
            dimension_semantics=("parallel", "arbitrary"), vmem_limit_bytes=_vmem_limit(est)),
        name="mlstm",
    )(p, p, p, scans, nw_cols)


def _outproj_kernel(ya_ref, za_ref, ym_ref, so_ref, sz_ref, ga_ref, gm_ref, x_ref,
                    wa_ref, wm_ref, wo_ref, fw_ref, out_ref):
    br_a = jnp.dot(ya_ref[...] * za_ref[...], wa_ref[...], preferred_element_type=F32)
    br_m = jnp.dot(so_ref[...] * ym_ref[...] * sz_ref[...], wm_ref[...],
                   preferred_element_type=F32)
    merged = ga_ref[...].astype(F32) * br_a + gm_ref[...].astype(F32) * br_m
    r = x_ref[...] + jnp.dot(merged.astype(BF16), wo_ref[...], preferred_element_type=F32)
    ms = jnp.mean(r * r, axis=-1, keepdims=True)
    out_ref[...] = r * lax.rsqrt(ms + RMS_EPS) * fw_ref[...]


def _outproj(ya, ym, p, x2d, wa, wm, wo, fw, *, tm):
    m = x2d.shape[0]
    row = lambda i: (i, 0)
    const = lambda i: (0, 0)
    seg = lambda name: pl.BlockSpec((tm, D_MODEL), lambda i: (i, _blk(name)))
    est = (2 * (7 * tm * D_MODEL * 2 + 2 * tm * D_MODEL * 4 + 3 * D_MODEL * D_MODEL * 2)
           + 6 * tm * D_MODEL * 4)
    return pl.pallas_call(
        _outproj_kernel,
        grid=(m // tm,),
        in_specs=[
            pl.BlockSpec((tm, D_MODEL), row), seg("a_z"),
            pl.BlockSpec((tm, D_MODEL), row), seg("m_o"), seg("m_z"),
            seg("g_a"), seg("g_m"),
            pl.BlockSpec((tm, D_MODEL), row),
            pl.BlockSpec((D_MODEL, D_MODEL), const),
            pl.BlockSpec((D_MODEL, D_MODEL), const),
            pl.BlockSpec((D_MODEL, D_MODEL), const),
            pl.BlockSpec((1, D_MODEL), const),
        ],
        out_specs=pl.BlockSpec((tm, D_MODEL), row),
        out_shape=jax.ShapeDtypeStruct((m, D_MODEL), F32),
        compiler_params=pltpu.CompilerParams(
            dimension_semantics=("parallel",), vmem_limit_bytes=_vmem_limit(est)),
        name="outproj",
    )(ya, p, ym, p, p, p, p, x2d, wa, wm, wo, fw)


def _layer(x, norm_w, w_in, b_in, sinks, conv_w, conv_b, mnorm_w, w_ba, w_bm, w_out, final_w):
    batch, seq, d = x.shape
    assert d == D_MODEL and seq % 512 == 0
    assert w_in.shape == (D_MODEL, _SRC["g_m"][0] + _SRC["g_m"][1])
    m = batch * seq
    x2d = x.reshape(m, d)

    def seg(a, name):
        off, width = _SRC[name]
        return a[..., off:off + width]

    def main_cols(a):
        def one(nm):
            cols = seg(a, nm)
            if nm == "a_q":
                cols = cols * (ATTN_HEAD_DIM ** -0.5 * LOG2_E)
                hd, g = ATTN_HEAD_DIM, ATTN_GROUPS
                heads = cols.reshape(cols.shape[:-1] + (ATTN_KV_HEADS // 2, 2, g, hd))
                cols = jnp.swapaxes(heads, -3, -2).reshape(cols.shape)
            return cols
        return jnp.concatenate([one(nm) for nm in _ORDER], axis=-1)

    w_main, w_gate = _prep_weights(w_in, rows=128)
    b_main = main_cols(b_in)[None, :]
    rep = LANES // MLSTM_HEADS
    b_gate = jnp.concatenate(
        [jnp.tile(seg(b_in, "m_i"), rep), jnp.tile(seg(b_in, "m_f"), rep)], axis=-1)[None, :]

    p, scans = _inproj(x2d, norm_w[None, :], w_main, b_main, w_gate, b_gate, conv_w,
                       conv_b[None, :], tm=512, seq=seq)
    ya = _attn(p, sinks, batch=batch, seq=seq, tq=512)
    ym = _mlstm(p, scans, mnorm_w, batch=batch, seq=seq, n_chunks=4)
    out = _outproj(ya, ym, p, x2d, w_ba.astype(BF16), w_bm.astype(BF16), w_out.astype(BF16),
                   final_w[None, :], tm=512)
    return out.reshape(batch, seq, d)


def kernel(x, norm_w, w_in, b_in, attn_sinks, conv_w, conv_b, mlstm_norm_w, w_branch_attn,
           w_branch_mlstm, w_out, final_norm_w):
    assert norm_w.shape[0] == 1, "single-layer pipeline: the final norm is fused into the layer"
    return _layer(x, norm_w[0], w_in[0], b_in[0], attn_sinks[0], conv_w[0], conv_b[0],
                  mlstm_norm_w[0], w_branch_attn[0], w_branch_mlstm[0], w_out[0], final_norm_w)
```

```python
import functools

import jax
import jax.numpy as jnp
from jax import lax
from jax.experimental import pallas as pl
from jax.experimental.pallas import tpu as pltpu

F32 = jnp.float32
BF16 = jnp.bfloat16

D_MODEL = 1024
ATTN_HEADS = 16
ATTN_KV_HEADS = 4
ATTN_GROUPS = ATTN_HEADS // ATTN_KV_HEADS
ATTN_HEAD_DIM = 64
ATTN_BLOCK = 128
ATTN_AUG_ROWS = 16
MLSTM_HEADS = 8
MLSTM_QK_DIM = 64
MLSTM_V_DIM = 128
MLSTM_CHUNK = 128
MLSTM_AUG_ROWS = 16
MLSTM_LOOKAHEAD = 3
ATTN_LOOKAHEAD = 2
CONV_WIDTH = 4
CONV_HALO = 8
M_INIT = -1e30
RMS_EPS = 1e-6
LANES = 128
LOG2_E = 1.4426950408889634

_SRC = dict(a_q=(0, 1024), a_k=(1024, 256), a_v=(1280, 256), a_z=(1536, 1024),
            m_q=(2560, 512), m_k=(3072, 512), m_v=(3584, 1024), m_i=(4608, 8), m_f=(4616, 8),
            m_o=(4624, 1024), m_z=(5648, 1024), g_a=(6672, 1024), g_m=(7696, 1024))
_ORDER = ("a_q", "a_z", "m_v", "m_o", "m_z", "g_a", "g_m", "m_q", "m_k", "a_k", "a_v")
_ACT = dict(a_z="silu", m_o="sigmoid", m_z="silu", g_a="sigmoid", g_m="sigmoid",
            m_q="conv", m_k="conv")
_OFF = {}
_o = 0
for _name in _ORDER:
    _OFF[_name] = _o
    _o += _SRC[_name][1]
N_MAIN = _o
N_CHUNK = 512
GATE_W = 2 * LANES
SCAN_W = 3 * LANES

V7X_VMEM_BYTES = 64 * 1024 * 1024


def _blk(name):
    off, width = _OFF[name], _SRC[name][1]
    assert off % width == 0
    return off // width


def _vmem_limit(estimate_bytes):
    return int(min(V7X_VMEM_BYTES - (4 << 20), estimate_bytes + (8 << 20)))


def _sigmoid(x):
    return 0.5 * jnp.tanh(0.5 * x) + 0.5


def _silu(x):
    return x * _sigmoid(x)


def _log_sigmoid(x):
    return jnp.minimum(x, 0.0) - jnp.log1p(jnp.exp(-jnp.abs(x)))


def _scan_time(x, combine, fill):
    t = lax.broadcasted_iota(jnp.int32, x.shape, 0)
    sh = 1
    while sh < x.shape[0]:
        x = combine(x, jnp.where(t >= sh, pltpu.roll(x, sh, 0), fill))
        sh *= 2
    return x


def _prep_kernel(w_ref, wm_ref, wg_ref):
    hd, g = ATTN_HEAD_DIM, ATTN_GROUPS

    def rows_t(start, width):
        return w_ref[start:start + width, :].T

    for name in _ORDER:
        src, width = _SRC[name]
        dst = _OFF[name]
        if name == "a_q":
            scale = hd ** -0.5 * LOG2_E
            for t in range(ATTN_HEADS // 2):
                lo = (t // g) * 2 * g + t % g
                pair = jnp.concatenate([w_ref[src + head * hd:src + (head + 1) * hd, :]
                                        for head in (lo, lo + g)], axis=0)
                wm_ref[:, dst + t * 2 * hd:dst + (t + 1) * 2 * hd] = (pair.T * scale).astype(BF16)
        else:
            for c in range(0, width, 2 * LANES):
                wm_ref[:, dst + c:dst + c + 2 * LANES] = rows_t(src + c, 2 * LANES).astype(BF16)
    for tile, name in enumerate(("m_i", "m_f")):
        src, width = _SRC[name]
        rep = jnp.concatenate([w_ref[src:src + width, :]] * (LANES // width), axis=0)
        wg_ref[:, tile * LANES:(tile + 1) * LANES] = rep.T.astype(BF16)


def _prep_weights(w_t, *, slab):
    n_src, d = w_t.shape
    return pl.pallas_call(
        _prep_kernel,
        grid=(d // slab,),
        in_specs=[pl.BlockSpec((n_src, slab), lambda i: (0, i))],
        out_specs=[pl.BlockSpec((slab, N_MAIN), lambda i: (i, 0)),
                   pl.BlockSpec((slab, GATE_W), lambda i: (i, 0))],
        out_shape=[jax.ShapeDtypeStruct((d, N_MAIN), BF16),
                   jax.ShapeDtypeStruct((d, GATE_W), BF16)],
        compiler_params=pltpu.CompilerParams(
            dimension_semantics=("parallel",),
            vmem_limit_bytes=_vmem_limit(2 * slab * (n_src * 4 + N_MAIN * 2 + GATE_W * 2))),
        name="prep_weights",
    )(w_t)


def _inproj_kernel(x_ref, nw_ref, w_ref, b_ref, wg_ref, bg_ref, cw_ref, cb_ref,
                   p_ref, s_ref, halo, *, tiles_per_seq):
    tm = x_ref.shape[0]
    L = MLSTM_CHUNK
    qkw = MLSTM_HEADS * MLSTM_QK_DIM

    @pl.when(pl.program_id(0) % tiles_per_seq == 0)
    def _():
        halo[...] = jnp.zeros(halo.shape, F32)

    x = x_ref[...]
    ms = jnp.mean(x * x, axis=-1, keepdims=True)
    h = (x * lax.rsqrt(ms + RMS_EPS) * nw_ref[...]).astype(BF16)

    def project(c, cw):
        return jnp.dot(h, w_ref[:, c:c + cw], preferred_element_type=F32) + b_ref[:, c:c + cw]

    u_halves = [project(_OFF[name], qkw) for name in ("m_q", "m_k")]
    g = jnp.dot(h, wg_ref[...], preferred_element_type=F32) + bg_ref[...]

    def shift_time(a, carry_in):
        rolled = pltpu.roll(a, 1, 0)
        row = lax.broadcasted_iota(jnp.int32, (CONV_HALO, a.shape[1]), 0)
        head = jnp.where(row == 0, carry_in, rolled[0:CONV_HALO, :])
        return jnp.concatenate([head, rolled[CONV_HALO:, :]], axis=0)

    def conv_half(u0):
        u = u_halves[u0 // qkw]
        cols = slice(u0, u0 + N_CHUNK)
        w = [cw_ref[j:j + 1, cols] for j in range(CONV_WIDTH)]
        prev = [halo[CONV_HALO - 1 - d:CONV_HALO - d, cols] for d in range(CONV_WIDTH - 1)]
        acc = w[0] * u
        carry = w[0] * prev[0]
        for j in range(1, CONV_WIDTH):
            acc = w[j] * u + shift_time(acc, carry)
            if j < CONV_WIDTH - 1:
                carry = sum(w[j - d] * prev[d] for d in range(j + 1))
        halo[:, cols] = u[tm - CONV_HALO:tm, :]
        qk = _silu(acc + cb_ref[:, cols])
        if u0 < qkw:
            qk = qk * (MLSTM_QK_DIM ** -0.5)
        c = _OFF["m_q"] + u0
        p_ref[:, c:c + N_CHUNK] = qk.astype(BF16)

    def gate_scans(t0):
        i_g = g[t0:t0 + L, 0:LANES]
        b = _scan_time(_log_sigmoid(g[t0:t0 + L, LANES:2 * LANES]), jnp.add, 0.0)
        r = i_g - b
        s_ref[t0:t0 + L, 0:LANES] = b
        s_ref[t0:t0 + L, LANES:2 * LANES] = r
        s_ref[t0:t0 + L, 2 * LANES:3 * LANES] = _scan_time(r, jnp.maximum, -jnp.inf)

    def plain_chunk(name, c):
        act = _ACT.get(name)
        acc = project(c, N_CHUNK)
        if act == "silu":
            acc = _silu(acc)
        elif act == "sigmoid":
            acc = _sigmoid(acc)
        p_ref[:, c:c + N_CHUNK] = acc.astype(BF16)

    vpu_work = ([functools.partial(conv_half, u0) for u0 in range(0, 2 * qkw, N_CHUNK)]
                + [functools.partial(gate_scans, t0) for t0 in range(0, tm, L)])
    chunks = []
    for name in _ORDER:
        off, width = _OFF[name], _SRC[name][1]
        if _ACT.get(name) == "conv" or name == "a_v":
            continue
        assert width % N_CHUNK == 0 or name == "a_k"
        chunks += [(name, c) for c in range(off, off + width, N_CHUNK)]
    per_group = 2
    while chunks:
        group, chunks = chunks[:per_group], chunks[per_group:]
        for name, c in group:
            plain_chunk(name, c)
        if vpu_work:
            vpu_work.pop(0)()
    assert not vpu_work


def _inproj(x2d, norm_w, w_main, b_main, w_gate, b_gate, conv_w, conv_b, *, tm, seq):
    m = x2d.shape[0]
    qkw2 = 2 * MLSTM_HEADS * MLSTM_QK_DIM
    est = (2 * tm * D_MODEL * 4 + D_MODEL * N_MAIN * 2 + 2 * tm * N_MAIN * 2
           + 2 * D_MODEL * GATE_W * 2 + 2 * tm * SCAN_W * 4 + (CONV_HALO + tm) * qkw2 * 4
           + 6 * tm * N_CHUNK * 4)
    const = lambda i: (0, 0)
    return pl.pallas_call(
        functools.partial(_inproj_kernel, tiles_per_seq=seq // tm),
        grid=(m // tm,),
        in_specs=[
            pl.BlockSpec((tm, D_MODEL), lambda i: (i, 0)),
            pl.BlockSpec((1, D_MODEL), const),
            pl.BlockSpec((D_MODEL, N_MAIN), const, pipeline_mode=pl.Buffered(1)),
            pl.BlockSpec((1, N_MAIN), const),
            pl.BlockSpec((D_MODEL, GATE_W), const),
            pl.BlockSpec((1, GATE_W), const),
            pl.BlockSpec((CONV_WIDTH, qkw2), const),
            pl.BlockSpec((1, qkw2), const),
        ],
        out_specs=[
            pl.BlockSpec((tm, N_MAIN), lambda i: (i, 0)),
            pl.BlockSpec((tm, SCAN_W), lambda i: (i, 0)),
        ],
        out_shape=[
            jax.ShapeDtypeStruct((m, N_MAIN), BF16),
            jax.ShapeDtypeStruct((m, SCAN_W), F32),
        ],
        scratch_shapes=[pltpu.VMEM((CONV_HALO, qkw2), F32)],
        compiler_params=pltpu.CompilerParams(
            dimension_semantics=("arbitrary",), vmem_limit_bytes=_vmem_limit(est)),
        name="inproj",
    )(x2d, norm_w, w_main, b_main, w_gate, b_gate, conv_w, conv_b)


def _attn_kernel(sink_ref, bias_ref, q_ref, kc_ref, kp_ref, vc_ref, vp_ref, o_ref, st_ring,
                 *, tq):
    n = pl.program_id(1)
    blk = ATTN_BLOCK
    hd = ATTN_HEAD_DIM
    ones_rows = (lax.broadcasted_iota(jnp.int32, (ATTN_AUG_ROWS, 2 * blk), 0) == 0).astype(BF16)
    lane_half = lax.broadcasted_iota(jnp.int32, (1, 2 * hd), 1) // hd
    blocks = []
    for s in range(tq // blk):
        rows = slice(s * blk, (s + 1) * blk)
        if s == 0:
            k_prev, v_prev = kp_ref[...], vp_ref[...]
            bias = bias_ref[jnp.where(n > 0, 0, 1)]
        else:
            prev_rows = slice((s - 1) * blk, s * blk)
            k_prev, v_prev = kc_ref[prev_rows, :], vc_ref[prev_rows, :]
            bias = bias_ref[0]
        keys = jnp.concatenate([k_prev, kc_ref[rows, :]], axis=0)
        vals = jnp.concatenate([v_prev, vc_ref[rows, :]], axis=0)
        vals_t = vals.astype(F32).T.astype(BF16)
        k_sel = [jnp.where(lane_half == (kv % 2),
                           keys[:, (kv // 2) * 2 * hd:(kv // 2 + 1) * 2 * hd], jnp.zeros((), BF16))
                 for kv in range(ATTN_KV_HEADS)]
        v_aug = [jnp.concatenate([vals_t[kv * hd:(kv + 1) * hd, :], ones_rows], axis=0)
                 for kv in range(ATTN_KV_HEADS)]
        blocks.append((rows, bias, k_sel, v_aug))

    jobs = [(s, kv) for s in range(tq // blk) for kv in range(ATTN_KV_HEADS)]

    def scores(s, kv):
        rows, _, k_sel, _ = blocks[s]
        tiles = [(kv // 2) * ATTN_GROUPS + j for j in range(ATTN_GROUPS)]
        q_g = jnp.concatenate(
            [q_ref[rows, t * 2 * hd:(t + 1) * 2 * hd] for t in tiles], axis=0)
        slot = jobs.index((s, kv)) % (ATTN_LOOKAHEAD + 1)
        st_ring[slot] = lax.dot_general(k_sel[kv], q_g, (((1,), (1,)), ((), ())),
                                        preferred_element_type=F32)
        return slot

    def finish(s, kv, slot):
        rows, bias, _, v_aug = blocks[s]
        sink = sink_ref[kv:kv + 1, :]
        st = st_ring[slot] + bias
        m = jnp.maximum(jnp.max(st, axis=0, keepdims=True), sink)
        p = jnp.exp2(st - m).astype(BF16)
        o_aug = jnp.dot(v_aug[kv], p, preferred_element_type=F32)
        den = o_aug[hd:hd + 1, :] + jnp.exp2(sink - m)
        o_t = o_aug[0:hd, :] / den
        pairs = [jnp.concatenate([o_t[:, (2 * j) * blk:(2 * j + 1) * blk],
                                  o_t[:, (2 * j + 1) * blk:(2 * j + 2) * blk]], axis=0).T
                 for j in range(ATTN_GROUPS // 2)]
        cols = slice(kv * ATTN_GROUPS * hd, (kv + 1) * ATTN_GROUPS * hd)
        o_ref[rows, cols] = jnp.concatenate(pairs, axis=1).astype(BF16)

    started = {}
    for idx, job in enumerate(jobs):
        for ahead in jobs[idx:idx + ATTN_LOOKAHEAD + 1]:
            if ahead not in started:
                started[ahead] = scores(*ahead)
        finish(*job, started.pop(job))


def _attn_bias():
    blk = ATTN_BLOCK
    key = jnp.arange(2 * blk)[:, None]
    qry = (jnp.arange(ATTN_GROUPS * blk) % blk)[None, :]
    in_window = ((key < blk) & (key > qry)) | ((key >= blk) & ((key - blk) <= qry))
    first = in_window & (key >= blk)
    return jnp.where(jnp.stack([in_window, first]), 0.0, -jnp.inf).astype(F32)


def _attn(p, sinks, *, batch, seq, tq):
    m = batch * seq
    nt = seq // tq
    sub = tq // ATTN_BLOCK
    kvw = ATTN_KV_HEADS * ATTN_HEAD_DIM
    gq = ATTN_GROUPS * ATTN_BLOCK
    cur = lambda b, n: b * nt + n
    prev = lambda b, n: jnp.maximum((b * nt + n) * sub - 1, 0)
    est = 2 * (tq * D_MODEL * 2 + 2 * tq * kvw * 2 + 2 * ATTN_BLOCK * kvw * 2
               + tq * D_MODEL * 2 + 2 * 2 * ATTN_BLOCK * gq * 4) + (8 << 20)
    sink_rows = jnp.repeat(
        (sinks.astype(F32) * LOG2_E).reshape(ATTN_KV_HEADS, ATTN_GROUPS), ATTN_BLOCK, axis=1)
    return pl.pallas_call(
        functools.partial(_attn_kernel, tq=tq),
        grid=(batch, nt),
        in_specs=[
            pl.BlockSpec((ATTN_KV_HEADS, gq), lambda b, n: (0, 0)),
            pl.BlockSpec((2, 2 * ATTN_BLOCK, gq), lambda b, n: (0, 0, 0)),
            pl.BlockSpec((tq, D_MODEL), lambda b, n: (cur(b, n), _blk("a_q"))),
            pl.BlockSpec((tq, kvw), lambda b, n: (cur(b, n), _blk("a_k"))),
            pl.BlockSpec((ATTN_BLOCK, kvw), lambda b, n: (prev(b, n), _blk("a_k"))),
            pl.BlockSpec((tq, kvw), lambda b, n: (cur(b, n), _blk("a_v"))),
            pl.BlockSpec((ATTN_BLOCK, kvw), lambda b, n: (prev(b, n), _blk("a_v"))),
        ],
        out_specs=pl.BlockSpec((tq, D_MODEL), lambda b, n: (cur(b, n), 0)),
        out_shape=jax.ShapeDtypeStruct((m, D_MODEL), BF16),
        scratch_shapes=[pltpu.VMEM((ATTN_LOOKAHEAD + 1, 2 * ATTN_BLOCK, gq), F32)],
        compiler_params=pltpu.CompilerParams(
            dimension_semantics=("parallel", "arbitrary"), vmem_limit_bytes=_vmem_limit(est)),
        name="swa_attn",
    )(sink_rows, _attn_bias(), p, p, p, p, p)


def _mlstm_kernel(q_ref, k_ref, v_ref, s_ref, nw_ref, y_ref, c_st, m_st, *, n_chunks):
    L = MLSTM_CHUNK
    H = MLSTM_HEADS
    dk = MLSTM_QK_DIM
    dv = MLSTM_V_DIM
    aug = MLSTM_AUG_ROWS

    @pl.when(pl.program_id(1) == 0)
    def _():
        c_st[...] = jnp.zeros(c_st.shape, F32)
        m_st[...] = jnp.full(m_st.shape, M_INIT, F32)

    s_idx = lax.broadcasted_iota(jnp.int32, (L, L), 0)
    t_idx = lax.broadcasted_iota(jnp.int32, (L, L), 1)
    causal_t = s_idx <= t_idx
    group = lax.broadcasted_iota(jnp.int32, (L, LANES), 1) // H
    ones_rows = (lax.broadcasted_iota(jnp.int32, (aug, L), 0) == 0).astype(F32)
    m_prev = m_st[...]

    gate_rows = []
    for c in range(n_chunks):
        rows = slice(c * L, (c + 1) * L)
        b = s_ref[rows, 0:LANES]
        r = s_ref[rows, LANES:2 * LANES]
        g_tot = b[L - 1:L, :]
        a = g_tot + r
        m_new = jnp.maximum(g_tot + m_prev, jnp.max(a, axis=0, keepdims=True))
        s_prev = jnp.exp(g_tot + m_prev - m_new)
        m_inter = b + m_prev
        m_row = jnp.maximum(m_inter, b + s_ref[rows, 2 * LANES:3 * LANES])
        packed = jnp.where(group == 0, b - m_row,
                 jnp.where(group == 1, jnp.exp(m_inter - m_row),
                 jnp.where(group == 2, jnp.exp(-m_row),
                 jnp.where(group == 3, jnp.exp(a - m_new),
                           jnp.broadcast_to(s_prev, (L, LANES))))))
        gate_rows.append((r, packed.T))
        m_prev = m_new
    m_st[...] = m_prev

    lane_half = lax.broadcasted_iota(jnp.int32, (1, LANES), 1) // dk

    def prepare(c, h):
        rows = slice(c * L, (c + 1) * L)
        pair = slice((h // 2) * 2 * dk, (h // 2 + 1) * 2 * dk)
        mine = lane_half == (h % 2)
        r, lane_rows = gate_rows[c]
        dm_row = lane_rows[h:h + 1, :]
        w_row = lane_rows[3 * H + h:3 * H + h + 1, :]
        q_h = jnp.where(mine, q_ref[rows, pair], jnp.zeros((), BF16))
        k_p = k_ref[rows, pair]
        v_t = v_ref[rows, h * dv:(h + 1) * dv].astype(F32).T
        v_aug = jnp.concatenate([v_t, ones_rows], axis=0)
        decay_t = jnp.exp(jnp.where(
            causal_t, jnp.broadcast_to(r[:, h:h + 1], (L, L)) + dm_row, -jnp.inf))
        return dict(q=q_h, k=k_p, v=v_aug.astype(BF16), vw=(v_aug * w_row).astype(BF16),
                    decay=decay_t, mine=mine)

    def read_out(c, h, job):
        c_h = c_st[h]
        prod = lax.dot_general(
            jnp.concatenate([job["k"], c_h.astype(BF16)], axis=0), job["q"],
            (((1,), (1,)), ((), ())), preferred_element_type=F32)
        return c_h, prod

    def finish(c, h, job, c_h, prod):
        rows = slice(c * L, (c + 1) * L)
        _, lane_rows = gate_rows[c]
        inter_row = lane_rows[H + h:H + h + 1, :]
        floor_row = lane_rows[2 * H + h:2 * H + h + 1, :]
        sp_row = lane_rows[4 * H + h:4 * H + h + 1, :]
        s_t = (prod[0:L, :] * job["decay"]).astype(BF16)
        num = (jnp.dot(job["v"], s_t, preferred_element_type=F32)
               + inter_row * prod[L:L + dv + aug, :])
        inv = 1.0 / jnp.maximum(jnp.abs(num[dv:dv + 1, :]), floor_row)
        h_t = num[0:dv, :] * inv
        upd = jnp.dot(job["vw"], job["k"], preferred_element_type=F32)
        c_st[h] = jnp.where(job["mine"], sp_row * c_h + upd, 0.0)
        ms = jnp.mean(h_t * h_t, axis=0, keepdims=True)
        y_t = h_t * lax.rsqrt(ms + RMS_EPS) * nw_ref[h * dv:(h + 1) * dv, :]
        y_ref[rows, h * dv:(h + 1) * dv] = y_t.T.astype(BF16)

    jobs = [(c, h) for c in range(n_chunks) for h in range(H)]
    assert MLSTM_LOOKAHEAD < H
    prepared = {job: prepare(*job) for job in jobs}
    started = {}
    for idx, job in enumerate(jobs):
        for ahead in jobs[idx:idx + MLSTM_LOOKAHEAD + 1]:
            if ahead not in started:
                started[ahead] = read_out(*ahead, prepared[ahead])
        finish(*job, prepared[job], *started.pop(job))


def _mlstm(p, scans, norm_w, *, batch, seq, n_chunks):
    m = batch * seq
    L = MLSTM_CHUNK
    tb = n_chunks * L
    nt = seq // tb
    qkw = MLSTM_HEADS * MLSTM_QK_DIM
    vw = MLSTM_HEADS * MLSTM_V_DIM
    cur = lambda b, n: b * nt + n
    nw_cols = jnp.broadcast_to(norm_w.astype(F32)[:, None], (vw, L))
    est = (2 * (2 * tb * qkw * 2 + tb * vw * 2 + tb * SCAN_W * 4 + tb * vw * 2)
           + 2 * vw * L * 4 + (16 << 20))
    return pl.pallas_call(
        functools.partial(_mlstm_kernel, n_chunks=n_chunks),
        grid=(batch, nt),
        in_specs=[
            pl.BlockSpec((tb, qkw), lambda b, n: (cur(b, n), _blk("m_q"))),
            pl.BlockSpec((tb, qkw), lambda b, n: (cur(b, n), _blk("m_k"))),
            pl.BlockSpec((tb, vw), lambda b, n: (cur(b, n), _blk("m_v"))),
            pl.BlockSpec((tb, SCAN_W), lambda b, n: (cur(b, n), 0)),
            pl.BlockSpec((vw, L), lambda b, n: (0, 0)),
        ],
        out_specs=pl.BlockSpec((tb, vw), lambda b, n: (cur(b, n), 0)),
        out_shape=jax.ShapeDtypeStruct((m, vw), BF16),
        scratch_shapes=[
            pltpu.VMEM((MLSTM_HEADS, MLSTM_V_DIM + MLSTM_AUG_ROWS, 2 * MLSTM_QK_DIM), F32),
            pltpu.VMEM((1, LANES), F32),
        ],
        compiler_params=pltpu.CompilerParams(
            dimension_semantics=("parallel", "arbitrary"), vmem_limit_bytes=_vmem_limit(est)),
        name="mlstm",
    )(p, p, p, scans, nw_cols)


def _outproj_kernel(ya_ref, za_ref, ym_ref, so_ref, sz_ref, ga_ref, gm_ref, x_ref,
                    wa_ref, wm_ref, wo_ref, fw_ref, out_ref):
    br_a = jnp.dot(ya_ref[...] * za_ref[...], wa_ref[...], preferred_element_type=F32)
    br_m = jnp.dot(so_ref[...] * ym_ref[...] * sz_ref[...], wm_ref[...],
                   preferred_element_type=F32)
    merged = ga_ref[...].astype(F32) * br_a + gm_ref[...].astype(F32) * br_m
    r = x_ref[...] + jnp.dot(merged.astype(BF16), wo_ref[...], preferred_element_type=F32)
    ms = jnp.mean(r * r, axis=-1, keepdims=True)
    out_ref[...] = r * lax.rsqrt(ms + RMS_EPS) * fw_ref[...]


def _outproj(ya, ym, p, x2d, wa, wm, wo, fw, *, tm):
    m = x2d.shape[0]
    row = lambda i: (i, 0)
    const = lambda i: (0, 0)
    seg = lambda name: pl.BlockSpec((tm, D_MODEL), lambda i: (i, _blk(name)))
    est = (2 * (7 * tm * D_MODEL * 2 + 2 * tm * D_MODEL * 4 + 3 * D_MODEL * D_MODEL * 2)
           + 6 * tm * D_MODEL * 4)
    return pl.pallas_call(
        _outproj_kernel,
        grid=(m // tm,),
        in_specs=[
            pl.BlockSpec((tm, D_MODEL), row), seg("a_z"),
            pl.BlockSpec((tm, D_MODEL), row), seg("m_o"), seg("m_z"),
            seg("g_a"), seg("g_m"),
            pl.BlockSpec((tm, D_MODEL), row),
            pl.BlockSpec((D_MODEL, D_MODEL), const),
            pl.BlockSpec((D_MODEL, D_MODEL), const),
            pl.BlockSpec((D_MODEL, D_MODEL), const),
            pl.BlockSpec((1, D_MODEL), const),
        ],
        out_specs=pl.BlockSpec((tm, D_MODEL), row),
        out_shape=jax.ShapeDtypeStruct((m, D_MODEL), F32),
        compiler_params=pltpu.CompilerParams(
            dimension_semantics=("parallel",), vmem_limit_bytes=_vmem_limit(est)),
        name="outproj",
    )(ya, p, ym, p, p, p, p, x2d, wa, wm, wo, fw)


def _layer(x, norm_w, w_in, b_in, sinks, conv_w, conv_b, mnorm_w, w_ba, w_bm, w_out, final_w):
    batch, seq, d = x.shape
    assert d == D_MODEL and seq % 512 == 0
    assert w_in.shape == (D_MODEL, _SRC["g_m"][0] + _SRC["g_m"][1])
    m = batch * seq
    x2d = x.reshape(m, d)

    def seg(a, name):
        off, width = _SRC[name]
        return a[..., off:off + width]

    def main_cols(a):
        def one(nm):
            cols = seg(a, nm)
            if nm == "a_q":
                cols = cols * (ATTN_HEAD_DIM ** -0.5 * LOG2_E)
                hd, g = ATTN_HEAD_DIM, ATTN_GROUPS
                heads = cols.reshape(cols.shape[:-1] + (ATTN_KV_HEADS // 2, 2, g, hd))
                cols = jnp.swapaxes(heads, -3, -2).reshape(cols.shape)
            return cols
        return jnp.concatenate([one(nm) for nm in _ORDER], axis=-1)

    w_main, w_gate = _prep_weights(w_in.T, slab=128)
    b_main = main_cols(b_in)[None, :]
    rep = LANES // MLSTM_HEADS
    b_gate = jnp.concatenate(
        [jnp.tile(seg(b_in, "m_i"), rep), jnp.tile(seg(b_in, "m_f"), rep)], axis=-1)[None, :]

    p, scans = _inproj(x2d, norm_w[None, :], w_main, b_main, w_gate, b_gate, conv_w,
                       conv_b[None, :], tm=512, seq=seq)
    ya = _attn(p, sinks, batch=batch, seq=seq, tq=512)
    ym = _mlstm(p, scans, mnorm_w, batch=batch, seq=seq, n_chunks=4)
    out = _outproj(ya, ym, p, x2d, w_ba.astype(BF16), w_bm.astype(BF16), w_out.astype(BF16),
                   final_w[None, :], tm=512)
    return out.reshape(batch, seq, d)


def kernel(x, norm_w, w_in, b_in, attn_sinks, conv_w, conv_b, mlstm_norm_w, w_branch_attn,
           w_branch_mlstm, w_out, final_norm_w):
    assert norm_w.shape[0] == 1, "single-layer pipeline: the final norm is fused into the layer"
    return _layer(x, norm_w[0], w_in[0], b_in[0], attn_sinks[0], conv_w[0], conv_b[0],
                  mlstm_norm_w[0], w_branch_attn[0], w_branch_mlstm[0], w_out[0], final_norm_w)
```

```python
import functools

import jax
import jax.numpy as jnp
from jax import lax
from jax.experimental import pallas as pl
from jax.experimental.pallas import tpu as pltpu

F32 = jnp.float32
BF16 = jnp.bfloat16

D_MODEL = 1024
ATTN_HEADS = 16
ATTN_KV_HEADS = 4
ATTN_GROUPS = ATTN_HEADS // ATTN_KV_HEADS
ATTN_HEAD_DIM = 64
ATTN_BLOCK = 128
ATTN_AUG_ROWS = 16
MLSTM_HEADS = 8
MLSTM_QK_DIM = 64
MLSTM_V_DIM = 128
MLSTM_CHUNK = 128
MLSTM_AUG_ROWS = 16
MLSTM_LOOKAHEAD = 3
ATTN_LOOKAHEAD = 2
CONV_WIDTH = 4
CONV_HALO = 8
M_INIT = -1e30
RMS_EPS = 1e-6
LANES = 128
LOG2_E = 1.4426950408889634

_SRC = dict(a_q=(0, 1024), a_k=(1024, 256), a_v=(1280, 256), a_z=(1536, 1024),
            m_q=(2560, 512), m_k=(3072, 512), m_v=(3584, 1024), m_i=(4608, 8), m_f=(4616, 8),
            m_o=(4624, 1024), m_z=(5648, 1024), g_a=(6672, 1024), g_m=(7696, 1024))
_ORDER = ("a_q", "a_z", "m_v", "m_o", "m_z", "g_a", "g_m", "m_q", "m_k", "a_k", "a_v")
_ACT = dict(a_z="silu", m_o="sigmoid", m_z="silu", g_a="sigmoid", g_m="sigmoid",
            m_q="conv", m_k="conv")
_OFF = {}
_o = 0
for _name in _ORDER:
    _OFF[_name] = _o
    _o += _SRC[_name][1]
N_MAIN = _o
_P_ORDER = ("a_q", "a_z", "m_v", "m_oz", "g_a", "g_m", "m_q", "m_k", "a_k", "a_v")
_P_WIDTH = dict({name: _SRC[name][1] for name in _ORDER}, m_oz=_SRC["m_o"][1])
_P_OFF = {}
_o = 0
for _name in _P_ORDER:
    _P_OFF[_name] = _o
    _o += _P_WIDTH[_name]
N_OUT = _o
N_CHUNK = 512
GATE_W = 2 * LANES
SCAN_W = 3 * LANES

V7X_VMEM_BYTES = 64 * 1024 * 1024


def _blk(name):
    off, width = _P_OFF[name], _P_WIDTH[name]
    assert off % width == 0
    return off // width


def _vmem_limit(estimate_bytes):
    return int(min(V7X_VMEM_BYTES - (4 << 20), estimate_bytes + (8 << 20)))


def _sigmoid(x):
    return 0.5 * jnp.tanh(0.5 * x) + 0.5


def _silu(x):
    return x * _sigmoid(x)


def _log_sigmoid(x):
    return jnp.minimum(x, 0.0) - jnp.log1p(jnp.exp(-jnp.abs(x)))


def _scan_time(x, combine, fill):
    t = lax.broadcasted_iota(jnp.int32, x.shape, 0)
    sh = 1
    while sh < x.shape[0]:
        x = combine(x, jnp.where(t >= sh, pltpu.roll(x, sh, 0), fill))
        sh *= 2
    return x


def _prep_kernel(w_ref, wm_ref, wg_ref):
    hd, g = ATTN_HEAD_DIM, ATTN_GROUPS

    def rows_t(start, width):
        return w_ref[start:start + width, :].T

    for name in _ORDER:
        src, width = _SRC[name]
        dst = _OFF[name]
        if name == "a_q":
            scale = hd ** -0.5 * LOG2_E
            for t in range(ATTN_HEADS // 2):
                lo = (t // g) * 2 * g + t % g
                pair = jnp.concatenate([w_ref[src + head * hd:src + (head + 1) * hd, :]
                                        for head in (lo, lo + g)], axis=0)
                wm_ref[:, dst + t * 2 * hd:dst + (t + 1) * 2 * hd] = (pair.T * scale).astype(BF16)
        else:
            for c in range(0, width, 2 * LANES):
                wm_ref[:, dst + c:dst + c + 2 * LANES] = rows_t(src + c, 2 * LANES).astype(BF16)
    for tile, name in enumerate(("m_i", "m_f")):
        src, width = _SRC[name]
        rep = jnp.concatenate([w_ref[src:src + width, :]] * (LANES // width), axis=0)
        wg_ref[:, tile * LANES:(tile + 1) * LANES] = rep.T.astype(BF16)


def _prep_weights(w_t, *, slab):
    n_src, d = w_t.shape
    return pl.pallas_call(
        _prep_kernel,
        grid=(d // slab,),
        in_specs=[pl.BlockSpec((n_src, slab), lambda i: (0, i))],
        out_specs=[pl.BlockSpec((slab, N_MAIN), lambda i: (i, 0)),
                   pl.BlockSpec((slab, GATE_W), lambda i: (i, 0))],
        out_shape=[jax.ShapeDtypeStruct((d, N_MAIN), BF16),
                   jax.ShapeDtypeStruct((d, GATE_W), BF16)],
        compiler_params=pltpu.CompilerParams(
            dimension_semantics=("parallel",),
            vmem_limit_bytes=_vmem_limit(2 * slab * (n_src * 4 + N_MAIN * 2 + GATE_W * 2))),
        name="prep_weights",
    )(w_t)


def _inproj_kernel(x_ref, nw_ref, w_ref, b_ref, wg_ref, bg_ref, cw_ref, cb_ref,
                   p_ref, s_ref, halo, *, tiles_per_seq):
    tm = x_ref.shape[0]
    L = MLSTM_CHUNK
    qkw = MLSTM_HEADS * MLSTM_QK_DIM

    @pl.when(pl.program_id(0) % tiles_per_seq == 0)
    def _():
        halo[...] = jnp.zeros(halo.shape, F32)

    x = x_ref[...]
    ms = jnp.mean(x * x, axis=-1, keepdims=True)
    h = (x * lax.rsqrt(ms + RMS_EPS) * nw_ref[...]).astype(BF16)

    def project(c, cw):
        return jnp.dot(h, w_ref[:, c:c + cw], preferred_element_type=F32) + b_ref[:, c:c + cw]

    u_halves = [project(_OFF[name], qkw) for name in ("m_q", "m_k")]
    g = jnp.dot(h, wg_ref[...], preferred_element_type=F32) + bg_ref[...]

    def shift_time(a, carry_in):
        rolled = pltpu.roll(a, 1, 0)
        row = lax.broadcasted_iota(jnp.int32, (CONV_HALO, a.shape[1]), 0)
        head = jnp.where(row == 0, carry_in, rolled[0:CONV_HALO, :])
        return jnp.concatenate([head, rolled[CONV_HALO:, :]], axis=0)

    def conv_half(u0):
        u = u_halves[u0 // qkw]
        cols = slice(u0, u0 + N_CHUNK)
        w = [cw_ref[j:j + 1, cols] for j in range(CONV_WIDTH)]
        prev = [halo[CONV_HALO - 1 - d:CONV_HALO - d, cols] for d in range(CONV_WIDTH - 1)]
        acc = w[0] * u
        carry = w[0] * prev[0]
        for j in range(1, CONV_WIDTH):
            acc = w[j] * u + shift_time(acc, carry)
            if j < CONV_WIDTH - 1:
                carry = sum(w[j - d] * prev[d] for d in range(j + 1))
        halo[:, cols] = u[tm - CONV_HALO:tm, :]
        qk = _silu(acc + cb_ref[:, cols])
        if u0 < qkw:
            qk = qk * (MLSTM_QK_DIM ** -0.5)
        c = _P_OFF["m_q"] + u0
        p_ref[:, c:c + N_CHUNK] = qk.astype(BF16)

    def gate_scans(t0):
        i_g = g[t0:t0 + L, 0:LANES]
        b = _scan_time(_log_sigmoid(g[t0:t0 + L, LANES:2 * LANES]), jnp.add, 0.0)
        r = i_g - b
        s_ref[t0:t0 + L, 0:LANES] = b
        s_ref[t0:t0 + L, LANES:2 * LANES] = r
        s_ref[t0:t0 + L, 2 * LANES:3 * LANES] = _scan_time(r, jnp.maximum, -jnp.inf)

    def plain_chunk(name, c):
        dst = _P_OFF[name] + c
        if name == "m_oz":
            acc = (_sigmoid(project(_OFF["m_o"] + c, N_CHUNK))
                   * _silu(project(_OFF["m_z"] + c, N_CHUNK)))
        else:
            acc = project(_OFF[name] + c, N_CHUNK)
            act = _ACT.get(name)
            if act == "silu":
                acc = _silu(acc)
            elif act == "sigmoid":
                acc = _sigmoid(acc)
        p_ref[:, dst:dst + N_CHUNK] = acc.astype(BF16)

    vpu_work = ([functools.partial(conv_half, u0) for u0 in range(0, 2 * qkw, N_CHUNK)]
                + [functools.partial(gate_scans, t0) for t0 in range(0, tm, L)])
    chunks = []
    for name in _P_ORDER:
        if _ACT.get(name) == "conv" or name == "a_v":
            continue
        width = _P_WIDTH[name]
        assert width % N_CHUNK == 0 or name == "a_k"
        assert _P_OFF["a_v"] - _P_OFF["a_k"] == _OFF["a_v"] - _OFF["a_k"] == _P_WIDTH["a_k"]
        chunks += [(name, c) for c in range(0, width, N_CHUNK)]
    per_group = 2
    while chunks:
        group, chunks = chunks[:per_group], chunks[per_group:]
        for name, c in group:
            plain_chunk(name, c)
        if vpu_work:
            vpu_work.pop(0)()
    assert not vpu_work


def _inproj(x2d, norm_w, w_main, b_main, w_gate, b_gate, conv_w, conv_b, *, tm, seq):
    m = x2d.shape[0]
    qkw2 = 2 * MLSTM_HEADS * MLSTM_QK_DIM
    est = (2 * tm * D_MODEL * 4 + D_MODEL * N_MAIN * 2 + 2 * tm * N_OUT * 2
           + 2 * D_MODEL * GATE_W * 2 + 2 * tm * SCAN_W * 4 + (CONV_HALO + tm) * qkw2 * 4
           + 6 * tm * N_CHUNK * 4)
    const = lambda i: (0, 0)
    return pl.pallas_call(
        functools.partial(_inproj_kernel, tiles_per_seq=seq // tm),
        grid=(m // tm,),
        in_specs=[
            pl.BlockSpec((tm, D_MODEL), lambda i: (i, 0)),
            pl.BlockSpec((1, D_MODEL), const),
            pl.BlockSpec((D_MODEL, N_MAIN), const, pipeline_mode=pl.Buffered(1)),
            pl.BlockSpec((1, N_MAIN), const),
            pl.BlockSpec((D_MODEL, GATE_W), const),
            pl.BlockSpec((1, GATE_W), const),
            pl.BlockSpec((CONV_WIDTH, qkw2), const),
            pl.BlockSpec((1, qkw2), const),
        ],
        out_specs=[
            pl.BlockSpec((tm, N_OUT), lambda i: (i, 0)),
            pl.BlockSpec((tm, SCAN_W), lambda i: (i, 0)),
        ],
        out_shape=[
            jax.ShapeDtypeStruct((m, N_OUT), BF16),
            jax.ShapeDtypeStruct((m, SCAN_W), F32),
        ],
        scratch_shapes=[pltpu.VMEM((CONV_HALO, qkw2), F32)],
        compiler_params=pltpu.CompilerParams(
            dimension_semantics=("arbitrary",), vmem_limit_bytes=_vmem_limit(est)),
        name="inproj",
    )(x2d, norm_w, w_main, b_main, w_gate, b_gate, conv_w, conv_b)


def _attn_kernel(sink_ref, bias_ref, q_ref, kc_ref, kp_ref, vc_ref, vp_ref, z_ref, o_ref, st_ring,
                 *, tq):
    n = pl.program_id(1)
    blk = ATTN_BLOCK
    hd = ATTN_HEAD_DIM
    ones_rows = (lax.broadcasted_iota(jnp.int32, (ATTN_AUG_ROWS, 2 * blk), 0) == 0).astype(BF16)
    lane_half = lax.broadcasted_iota(jnp.int32, (1, 2 * hd), 1) // hd
    blocks = []
    for s in range(tq // blk):
        rows = slice(s * blk, (s + 1) * blk)
        if s == 0:
            k_prev, v_prev = kp_ref[...], vp_ref[...]
            bias = bias_ref[jnp.where(n > 0, 0, 1)]
        else:
            prev_rows = slice((s - 1) * blk, s * blk)
            k_prev, v_prev = kc_ref[prev_rows, :], vc_ref[prev_rows, :]
            bias = bias_ref[0]
        keys = jnp.concatenate([k_prev, kc_ref[rows, :]], axis=0)
        vals = jnp.concatenate([v_prev, vc_ref[rows, :]], axis=0)
        vals_t = vals.astype(F32).T.astype(BF16)
        k_sel = [jnp.where(lane_half == (kv % 2),
                           keys[:, (kv // 2) * 2 * hd:(kv // 2 + 1) * 2 * hd], jnp.zeros((), BF16))
                 for kv in range(ATTN_KV_HEADS)]
        v_aug = [jnp.concatenate([vals_t[kv * hd:(kv + 1) * hd, :], ones_rows], axis=0)
                 for kv in range(ATTN_KV_HEADS)]
        blocks.append((rows, bias, k_sel, v_aug))

    jobs = [(s, kv) for s in range(tq // blk) for kv in range(ATTN_KV_HEADS)]

    def scores(s, kv):
        rows, _, k_sel, _ = blocks[s]
        tiles = [(kv // 2) * ATTN_GROUPS + j for j in range(ATTN_GROUPS)]
        q_g = jnp.concatenate(
            [q_ref[rows, t * 2 * hd:(t + 1) * 2 * hd] for t in tiles], axis=0)
        slot = jobs.index((s, kv)) % (ATTN_LOOKAHEAD + 1)
        st_ring[slot] = lax.dot_general(k_sel[kv], q_g, (((1,), (1,)), ((), ())),
                                        preferred_element_type=F32)
        return slot

    def finish(s, kv, slot):
        rows, bias, _, v_aug = blocks[s]
        sink = sink_ref[kv:kv + 1, :]
        st = st_ring[slot] + bias
        m = jnp.maximum(jnp.max(st, axis=0, keepdims=True), sink)
        p = jnp.exp2(st - m).astype(BF16)
        o_aug = jnp.dot(v_aug[kv], p, preferred_element_type=F32)
        den = o_aug[hd:hd + 1, :] + jnp.exp2(sink - m)
        o_t = o_aug[0:hd, :] / den
        pairs = [jnp.concatenate([o_t[:, (2 * j) * blk:(2 * j + 1) * blk],
                                  o_t[:, (2 * j + 1) * blk:(2 * j + 2) * blk]], axis=0).T
                 for j in range(ATTN_GROUPS // 2)]
        cols = slice(kv * ATTN_GROUPS * hd, (kv + 1) * ATTN_GROUPS * hd)
        o_ref[rows, cols] = (jnp.concatenate(pairs, axis=1) * z_ref[rows, cols]).astype(BF16)

    started = {}
    for idx, job in enumerate(jobs):
        for ahead in jobs[idx:idx + ATTN_LOOKAHEAD + 1]:
            if ahead not in started:
                started[ahead] = scores(*ahead)
        finish(*job, started.pop(job))


def _attn_bias():
    blk = ATTN_BLOCK
    key = jnp.arange(2 * blk)[:, None]
    qry = (jnp.arange(ATTN_GROUPS * blk) % blk)[None, :]
    in_window = ((key < blk) & (key > qry)) | ((key >= blk) & ((key - blk) <= qry))
    first = in_window & (key >= blk)
    return jnp.where(jnp.stack([in_window, first]), 0.0, -jnp.inf).astype(F32)


def _attn(p, sinks, *, batch, seq, tq):
    m = batch * seq
    nt = seq // tq
    sub = tq // ATTN_BLOCK
    kvw = ATTN_KV_HEADS * ATTN_HEAD_DIM
    gq = ATTN_GROUPS * ATTN_BLOCK
    cur = lambda b, n: b * nt + n
    prev = lambda b, n: jnp.maximum((b * nt + n) * sub - 1, 0)
    est = 2 * (2 * tq * D_MODEL * 2 + 2 * tq * kvw * 2 + 2 * ATTN_BLOCK * kvw * 2
               + tq * D_MODEL * 2 + 2 * 2 * ATTN_BLOCK * gq * 4) + (8 << 20)
    sink_rows = jnp.repeat(
        (sinks.astype(F32) * LOG2_E).reshape(ATTN_KV_HEADS, ATTN_GROUPS), ATTN_BLOCK, axis=1)
    return pl.pallas_call(
        functools.partial(_attn_kernel, tq=tq),
        grid=(batch, nt),
        in_specs=[
            pl.BlockSpec((ATTN_KV_HEADS, gq), lambda b, n: (0, 0)),
            pl.BlockSpec((2, 2 * ATTN_BLOCK, gq), lambda b, n: (0, 0, 0)),
            pl.BlockSpec((tq, D_MODEL), lambda b, n: (cur(b, n), _blk("a_q"))),
            pl.BlockSpec((tq, kvw), lambda b, n: (cur(b, n), _blk("a_k"))),
            pl.BlockSpec((ATTN_BLOCK, kvw), lambda b, n: (prev(b, n), _blk("a_k"))),
            pl.BlockSpec((tq, kvw), lambda b, n: (cur(b, n), _blk("a_v"))),
            pl.BlockSpec((ATTN_BLOCK, kvw), lambda b, n: (prev(b, n), _blk("a_v"))),
            pl.BlockSpec((tq, D_MODEL), lambda b, n: (cur(b, n), _blk("a_z"))),
        ],
        out_specs=pl.BlockSpec((tq, D_MODEL), lambda b, n: (cur(b, n), 0)),
        out_shape=jax.ShapeDtypeStruct((m, D_MODEL), BF16),
        scratch_shapes=[pltpu.VMEM((ATTN_LOOKAHEAD + 1, 2 * ATTN_BLOCK, gq), F32)],
        compiler_params=pltpu.CompilerParams(
            dimension_semantics=("parallel", "arbitrary"), vmem_limit_bytes=_vmem_limit(est)),
        name="swa_attn",
    )(sink_rows, _attn_bias(), p, p, p, p, p, p)


def _mlstm_kernel(q_ref, k_ref, v_ref, s_ref, oz_ref, nw_ref, y_ref, c_st, m_st, *, n_chunks):
    L = MLSTM_CHUNK
    H = MLSTM_HEADS
    dk = MLSTM_QK_DIM
    dv = MLSTM_V_DIM
    aug = MLSTM_AUG_ROWS

    @pl.when(pl.program_id(1) == 0)
    def _():
        c_st[...] = jnp.zeros(c_st.shape, F32)
        m_st[...] = jnp.full(m_st.shape, M_INIT, F32)

    s_idx = lax.broadcasted_iota(jnp.int32, (L, L), 0)
    t_idx = lax.broadcasted_iota(jnp.int32, (L, L), 1)
    causal_t = s_idx <= t_idx
    group = lax.broadcasted_iota(jnp.int32, (L, LANES), 1) // H
    ones_rows = (lax.broadcasted_iota(jnp.int32, (aug, L), 0) == 0).astype(F32)
    m_prev = m_st[...]

    gate_rows = []
    for c in range(n_chunks):
        rows = slice(c * L, (c + 1) * L)
        b = s_ref[rows, 0:LANES]
        r = s_ref[rows, LANES:2 * LANES]
        g_tot = b[L - 1:L, :]
        a = g_tot + r
        m_new = jnp.maximum(g_tot + m_prev, jnp.max(a, axis=0, keepdims=True))
        s_prev = jnp.exp(g_tot + m_prev - m_new)
        m_inter = b + m_prev
        m_row = jnp.maximum(m_inter, b + s_ref[rows, 2 * LANES:3 * LANES])
        packed = jnp.where(group == 0, b - m_row,
                 jnp.where(group == 1, jnp.exp(m_inter - m_row),
                 jnp.where(group == 2, jnp.exp(-m_row),
                 jnp.where(group == 3, jnp.exp(a - m_new),
                           jnp.broadcast_to(s_prev, (L, LANES))))))
        gate_rows.append((r, packed.T))
        m_prev = m_new
    m_st[...] = m_prev

    lane_half = lax.broadcasted_iota(jnp.int32, (1, LANES), 1) // dk

    def prepare(c, h):
        rows = slice(c * L, (c + 1) * L)
        pair = slice((h // 2) * 2 * dk, (h // 2 + 1) * 2 * dk)
        mine = lane_half == (h % 2)
        r, lane_rows = gate_rows[c]
        dm_row = lane_rows[h:h + 1, :]
        w_row = lane_rows[3 * H + h:3 * H + h + 1, :]
        q_h = jnp.where(mine, q_ref[rows, pair], jnp.zeros((), BF16))
        k_p = k_ref[rows, pair]
        v_t = v_ref[rows, h * dv:(h + 1) * dv].astype(F32).T
        v_aug = jnp.concatenate([v_t, ones_rows], axis=0)
        decay_t = jnp.exp(jnp.where(
            causal_t, jnp.broadcast_to(r[:, h:h + 1], (L, L)) + dm_row, -jnp.inf))
        return dict(q=q_h, k=k_p, v=v_aug.astype(BF16), vw=(v_aug * w_row).astype(BF16),
                    decay=decay_t, mine=mine)

    def read_out(c, h, job):
        c_h = c_st[h]
        prod = lax.dot_general(
            jnp.concatenate([job["k"], c_h.astype(BF16)], axis=0), job["q"],
            (((1,), (1,)), ((), ())), preferred_element_type=F32)
        return c_h, prod

    def finish(c, h, job, c_h, prod):
        rows = slice(c * L, (c + 1) * L)
        _, lane_rows = gate_rows[c]
        inter_row = lane_rows[H + h:H + h + 1, :]
        floor_row = lane_rows[2 * H + h:2 * H + h + 1, :]
        sp_row = lane_rows[4 * H + h:4 * H + h + 1, :]
        s_t = (prod[0:L, :] * job["decay"]).astype(BF16)
        num = (jnp.dot(job["v"], s_t, preferred_element_type=F32)
               + inter_row * prod[L:L + dv + aug, :])
        inv = 1.0 / jnp.maximum(jnp.abs(num[dv:dv + 1, :]), floor_row)
        h_t = num[0:dv, :] * inv
        upd = jnp.dot(job["vw"], job["k"], preferred_element_type=F32)
        c_st[h] = jnp.where(job["mine"], sp_row * c_h + upd, 0.0)
        ms = jnp.mean(h_t * h_t, axis=0, keepdims=True)
        y_t = h_t * lax.rsqrt(ms + RMS_EPS) * nw_ref[h * dv:(h + 1) * dv, :]
        y_ref[rows, h * dv:(h + 1) * dv] = (y_t.T * oz_ref[rows, h * dv:(h + 1) * dv]).astype(BF16)

    jobs = [(c, h) for c in range(n_chunks) for h in range(H)]
    assert MLSTM_LOOKAHEAD < H
    prepared = {job: prepare(*job) for job in jobs}
    started = {}
    for idx, job in enumerate(jobs):
        for ahead in jobs[idx:idx + MLSTM_LOOKAHEAD + 1]:
            if ahead not in started:
                started[ahead] = read_out(*ahead, prepared[ahead])
        finish(*job, prepared[job], *started.pop(job))


def _mlstm(p, scans, norm_w, *, batch, seq, n_chunks):
    m = batch * seq
    L = MLSTM_CHUNK
    tb = n_chunks * L
    nt = seq // tb
    qkw = MLSTM_HEADS * MLSTM_QK_DIM
    vw = MLSTM_HEADS * MLSTM_V_DIM
    cur = lambda b, n: b * nt + n
    nw_cols = jnp.broadcast_to(norm_w.astype(F32)[:, None], (vw, L))
    est = (2 * (2 * tb * qkw * 2 + 2 * tb * vw * 2 + tb * SCAN_W * 4 + tb * vw * 2)
           + 2 * vw * L * 4 + (16 << 20))
    return pl.pallas_call(
        functools.partial(_mlstm_kernel, n_chunks=n_chunks),
        grid=(batch, nt),
        in_specs=[
            pl.BlockSpec((tb, qkw), lambda b, n: (cur(b, n), _blk("m_q"))),
            pl.BlockSpec((tb, qkw), lambda b, n: (cur(b, n), _blk("m_k"))),
            pl.BlockSpec((tb, vw), lambda b, n: (cur(b, n), _blk("m_v"))),
            pl.BlockSpec((tb, SCAN_W), lambda b, n: (cur(b, n), 0)),
            pl.BlockSpec((tb, vw), lambda b, n: (cur(b, n), _blk("m_oz"))),
            pl.BlockSpec((vw, L), lambda b, n: (0, 0)),
        ],
        out_specs=pl.BlockSpec((tb, vw), lambda b, n: (cur(b, n), 0)),
        out_shape=jax.ShapeDtypeStruct((m, vw), BF16),
        scratch_shapes=[
            pltpu.VMEM((MLSTM_HEADS, MLSTM_V_DIM + MLSTM_AUG_ROWS, 2 * MLSTM_QK_DIM), F32),
            pltpu.VMEM((1, LANES), F32),
        ],
        compiler_params=pltpu.CompilerParams(
            dimension_semantics=("parallel", "arbitrary"), vmem_limit_bytes=_vmem_limit(est)),
        name="mlstm",
    )(p, p, p, scans, p, nw_cols)


def _outproj_kernel(ya_ref, ym_ref, ga_ref, gm_ref, x_ref, wa_ref, wm_ref, wo_ref, fw_ref,
                    out_ref):
    br_a = jnp.dot(ya_ref[...], wa_ref[...], preferred_element_type=F32)
    br_m = jnp.dot(ym_ref[...], wm_ref[...], preferred_element_type=F32)
    merged = ga_ref[...].astype(F32) * br_a + gm_ref[...].astype(F32) * br_m
    r = x_ref[...] + jnp.dot(merged.astype(BF16), wo_ref[...], preferred_element_type=F32)
    ms = jnp.mean(r * r, axis=-1, keepdims=True)
    out_ref[...] = r * lax.rsqrt(ms + RMS_EPS) * fw_ref[...]


def _outproj(ya, ym, p, x2d, wa, wm, wo, fw, *, tm):
    m = x2d.shape[0]
    row = lambda i: (i, 0)
    const = lambda i: (0, 0)
    seg = lambda name: pl.BlockSpec((tm, D_MODEL), lambda i: (i, _blk(name)))
    est = (2 * (4 * tm * D_MODEL * 2 + 2 * tm * D_MODEL * 4 + 3 * D_MODEL * D_MODEL * 2)
           + 6 * tm * D_MODEL * 4)
    return pl.pallas_call(
        _outproj_kernel,
        grid=(m // tm,),
        in_specs=[
            pl.BlockSpec((tm, D_MODEL), row),
            pl.BlockSpec((tm, D_MODEL), row),
            seg("g_a"), seg("g_m"),
            pl.BlockSpec((tm, D_MODEL), row),
            pl.BlockSpec((D_MODEL, D_MODEL), const),
            pl.BlockSpec((D_MODEL, D_MODEL), const),
            pl.BlockSpec((D_MODEL, D_MODEL), const),
            pl.BlockSpec((1, D_MODEL), const),
        ],
        out_specs=pl.BlockSpec((tm, D_MODEL), row),
        out_shape=jax.ShapeDtypeStruct((m, D_MODEL), F32),
        compiler_params=pltpu.CompilerParams(
            dimension_semantics=("parallel",), vmem_limit_bytes=_vmem_limit(est)),
        name="outproj",
    )(ya, ym, p, p, x2d, wa, wm, wo, fw)


def _layer(x, norm_w, w_in, b_in, sinks, conv_w, conv_b, mnorm_w, w_ba, w_bm, w_out, final_w):
    batch, seq, d = x.shape
    assert d == D_MODEL and seq % 512 == 0
    assert w_in.shape == (D_MODEL, _SRC["g_m"][0] + _SRC["g_m"][1])
    m = batch * seq
    x2d = x.reshape(m, d)

    def seg(a, name):
        off, width = _SRC[name]
        return a[..., off:off + width]

    def main_cols(a):
        def one(nm):
            cols = seg(a, nm)
            if nm == "a_q":
                cols = cols * (ATTN_HEAD_DIM ** -0.5 * LOG2_E)
                hd, g = ATTN_HEAD_DIM, ATTN_GROUPS
                heads = cols.reshape(cols.shape[:-1] + (ATTN_KV_HEADS // 2, 2, g, hd))
                cols = jnp.swapaxes(heads, -3, -2).reshape(cols.shape)
            return cols
        return jnp.concatenate([one(nm) for nm in _ORDER], axis=-1)

    w_main, w_gate = _prep_weights(w_in.T, slab=128)
    b_main = main_cols(b_in)[None, :]
    rep = LANES // MLSTM_HEADS
    b_gate = jnp.concatenate(
        [jnp.tile(seg(b_in, "m_i"), rep), jnp.tile(seg(b_in, "m_f"), rep)], axis=-1)[None, :]

    p, scans = _inproj(x2d, norm_w[None, :], w_main, b_main, w_gate, b_gate, conv_w,
                       conv_b[None, :], tm=512, seq=seq)
    ya = _attn(p, sinks, batch=batch, seq=seq, tq=512)
    ym = _mlstm(p, scans, mnorm_w, batch=batch, seq=seq, n_chunks=4)
    out = _outproj(ya, ym, p, x2d, w_ba.astype(BF16), w_bm.astype(BF16), w_out.astype(BF16),
                   final_w[None, :], tm=512)
    return out.reshape(batch, seq, d)


def kernel(x, norm_w, w_in, b_in, attn_sinks, conv_w, conv_b, mlstm_norm_w, w_branch_attn,
           w_branch_mlstm, w_out, final_norm_w):
    assert norm_w.shape[0] == 1, "single-layer pipeline: the final norm is fused into the layer"
    return _layer(x, norm_w[0], w_in[0], b_in[0], attn_sinks[0], conv_w[0], conv_b[0],
                  mlstm_norm_w[0], w_branch_attn[0], w_branch_mlstm[0], w_out[0], final_norm_w)
```
